```python
import math
import jax, jax.numpy as jnp
from jax import lax
import numpy as np

D_MODEL = 1024
BATCH = 2
SEQ = 8192
DEPTH = 1

PLE_DIM = 256
D_FF = 2816
MLA_HEADS = 8
MLA_NOPE = 64
MLA_ROPE = 32
MLA_QK = MLA_NOPE + MLA_ROPE
MLA_V = 64
MLA_Q_RANK = 256
MLA_KV_RANK = 128
ROPE_THETA = 10000.0
DIFF_HEADS = 4
DIFF_QK = 64
DIFF_V = 2 * DIFF_QK
MIX_WIDTH = MLA_HEADS * MLA_V + DIFF_HEADS * DIFF_V
IN_SIZES = (MLA_Q_RANK, MLA_KV_RANK, MLA_ROPE,
            DIFF_HEADS * 2 * DIFF_QK, DIFF_HEADS * 2 * DIFF_QK, DIFF_HEADS * DIFF_V)
N_IN = sum(IN_SIZES)
IN_OFFSETS = [int(v) for v in np.cumsum(IN_SIZES)[:-1]]
Q_BLOCK = 128
EPS = 1e-6

kernel_name = "hybrid_mla_diffattn_macaron_encoder"


def rms_norm(x, g):
    xf = x.astype(jnp.float32)
    y = xf * lax.rsqrt(jnp.mean(xf * xf, axis=-1, keepdims=True) + EPS)
    return (y * g.astype(jnp.float32)).astype(x.dtype)


def swiglu(u, w_gate, w_up, w_down):
    return (jax.nn.silu(u @ w_gate) * (u @ w_up)) @ w_down


def rope_tables(S, dtype):
    pos = jnp.arange(S, dtype=jnp.float32)
    inv = ROPE_THETA ** (-jnp.arange(0, MLA_ROPE, 2, dtype=jnp.float32) / MLA_ROPE)
    ang = pos[:, None] * inv[None, :]
    ang = jnp.concatenate([ang, ang], axis=-1)
    return jnp.cos(ang).astype(dtype), jnp.sin(ang).astype(dtype)


def apply_rope(x, cos, sin):
    half = x.shape[-1] // 2
    rot = jnp.concatenate([-x[..., half:], x[..., :half]], axis=-1)
    return x * cos + rot * sin


def sweep_query_blocks(block_fn, qs):
    B, H, S, _ = qs[0].shape
    nb = S // Q_BLOCK
    blocks = tuple(q.reshape(B, H, nb, Q_BLOCK, q.shape[-1]).transpose(2, 0, 1, 3, 4) for q in qs)
    starts = jnp.arange(nb, dtype=jnp.int32) * Q_BLOCK
    out = lax.map(lambda a: block_fn(a[0], *a[1]), (starts, blocks))
    dv = out.shape[-1]
    return out.transpose(1, 0, 3, 2, 4).reshape(B, S, H * dv)


def mla_mixer(q_lat, kv_lat, k_rope, g_q_lat, w_q_up, g_kv_lat, w_kv_up, g_q, g_k, cos, sin):
    B, S, _ = q_lat.shape
    q = (rms_norm(q_lat, g_q_lat) @ w_q_up).reshape(B, S, MLA_HEADS, MLA_QK)
    kv = (rms_norm(kv_lat, g_kv_lat) @ w_kv_up).reshape(B, S, MLA_HEADS, MLA_NOPE + MLA_V)
    k_nope, v = kv[..., :MLA_NOPE], kv[..., MLA_NOPE:]
    q_nope = rms_norm(q[..., :MLA_NOPE], g_q[:MLA_NOPE])
    q_rot = apply_rope(rms_norm(q[..., MLA_NOPE:], g_q[MLA_NOPE:]), cos[:, None, :], sin[:, None, :])
    k_nope = rms_norm(k_nope, g_k[:MLA_NOPE])
    k_rot = apply_rope(rms_norm(k_rope, g_k[MLA_NOPE:]), cos, sin)
    q = jnp.concatenate([q_nope, q_rot], axis=-1).transpose(0, 2, 1, 3)
    k = jnp.concatenate([k_nope, jnp.broadcast_to(k_rot[:, :, None, :], (B, S, MLA_HEADS, MLA_ROPE))],
                        axis=-1).transpose(0, 2, 1, 3)
    v = v.transpose(0, 2, 1, 3)
    scale = MLA_QK ** -0.5

    def block(start, qb):
        s = jnp.einsum('bhqd,bhkd->bhqk', qb, k).astype(jnp.float32) * scale
        a = jax.nn.softmax(s, axis=-1)
        return jnp.einsum('bhqk,bhkd->bhqd', a.astype(v.dtype), v)

    return sweep_query_blocks(block, (q,))


def diff_mixer(qd, kd, vd, g_q, g_k, lq1, lk1, lq2, lk2, g_sub, slopes, lambda_init):
    B, S, _ = qd.shape
    q = rms_norm(qd.reshape(B, S, DIFF_HEADS, 2, DIFF_QK), g_q)
    k = rms_norm(kd.reshape(B, S, DIFF_HEADS, 2, DIFF_QK), g_k)
    q1 = q[:, :, :, 0].transpose(0, 2, 1, 3)
    q2 = q[:, :, :, 1].transpose(0, 2, 1, 3)
    k1 = k[:, :, :, 0].transpose(0, 2, 1, 3)
    k2 = k[:, :, :, 1].transpose(0, 2, 1, 3)
    v = vd.reshape(B, S, DIFF_HEADS, DIFF_V).transpose(0, 2, 1, 3)
    lam = (jnp.exp(jnp.sum(lq1.astype(jnp.float32) * lk1.astype(jnp.float32)))
           - jnp.exp(jnp.sum(lq2.astype(jnp.float32) * lk2.astype(jnp.float32))) + lambda_init)
    scale = DIFF_QK ** -0.5
    pos_k = jnp.arange(S, dtype=jnp.int32)

    def block(start, q1b, q2b):
        pos_q = start + jnp.arange(Q_BLOCK, dtype=jnp.int32)
        dist = jnp.abs(pos_q[:, None] - pos_k[None, :]).astype(jnp.float32)
        bias = -slopes[:, None, None] * dist
        a1 = jax.nn.softmax(jnp.einsum('bhqd,bhkd->bhqk', q1b, k1).astype(jnp.float32) * scale + bias, axis=-1)
        a2 = jax.nn.softmax(jnp.einsum('bhqd,bhkd->bhqk', q2b, k2).astype(jnp.float32) * scale + bias, axis=-1)
        a = a1 - lam * a2
        return jnp.einsum('bhqk,bhkd->bhqd', a.astype(v.dtype), v)

    o = sweep_query_blocks(block, (q1, q2)).reshape(B, S, DIFF_HEADS, DIFF_V)
    o = rms_norm(o, g_sub) * (1.0 - lambda_init)
    return o.reshape(B, S, DIFF_HEADS * DIFF_V)


def setup_inputs(seed: int = 0) -> dict:
    key = jax.random.key(seed)
    ks = jax.random.split(key, 32)
    f32 = jnp.float32

    def w(k, shape, fan_in):
        return jax.random.normal(k, shape, f32) * (fan_in ** -0.5)

    def gain(k, n):
        return 1.0 + 0.02 * jax.random.normal(k, (DEPTH, n), f32)

    return {
        "x": jax.random.normal(ks[0], (BATCH, SEQ, D_MODEL), f32),
        "p": jax.random.normal(ks[1], (DEPTH, BATCH, SEQ, PLE_DIM), f32),
        "g_ffn1": gain(ks[2], D_MODEL),
        "w_ffn1_gate": w(ks[3], (DEPTH, D_MODEL, D_FF), D_MODEL),
        "w_ffn1_up": w(ks[4], (DEPTH, D_MODEL, D_FF), D_MODEL),
        "w_ffn1_down": w(ks[5], (DEPTH, D_FF, D_MODEL), D_FF),
        "g_mix": gain(ks[6], D_MODEL),
        "w_in": w(ks[7], (DEPTH, D_MODEL, N_IN), D_MODEL),
        "g_q_lat": gain(ks[8], MLA_Q_RANK),
        "w_q_up": w(ks[9], (DEPTH, MLA_Q_RANK, MLA_HEADS * MLA_QK), MLA_Q_RANK),
        "g_kv_lat": gain(ks[10], MLA_KV_RANK),
        "w_kv_up": w(ks[11], (DEPTH, MLA_KV_RANK, MLA_HEADS * (MLA_NOPE + MLA_V)), MLA_KV_RANK),
        "g_mla_q": gain(ks[12], MLA_QK),
        "g_mla_k": gain(ks[13], MLA_QK),
        "g_diff_q": gain(ks[14], DIFF_QK),
        "g_diff_k": gain(ks[15], DIFF_QK),
        "lambda_q1": 0.1 * jax.random.normal(ks[16], (DEPTH, DIFF_QK), f32),
        "lambda_k1": 0.1 * jax.random.normal(ks[17], (DEPTH, DIFF_QK), f32),
        "lambda_q2": 0.1 * jax.random.normal(ks[18], (DEPTH, DIFF_QK), f32),
        "lambda_k2": 0.1 * jax.random.normal(ks[19], (DEPTH, DIFF_QK), f32),
        "g_diff_sub": gain(ks[20], DIFF_V),
        "w_out": w(ks[21], (DEPTH, MIX_WIDTH, D_MODEL), MIX_WIDTH),
        "g_ffn2": gain(ks[22], D_MODEL),
        "w_ffn2_gate": w(ks[23], (DEPTH, D_MODEL, D_FF), D_MODEL),
        "w_ffn2_up": w(ks[24], (DEPTH, D_MODEL, D_FF), D_MODEL),
        "w_ffn2_down": w(ks[25], (DEPTH, D_FF, D_MODEL), D_FF),
        "g_ple_in": gain(ks[26], D_MODEL),
        "w_ple_gate": w(ks[27], (DEPTH, D_MODEL, D_MODEL), D_MODEL),
        "b_ple_gate": 0.02 * jax.random.normal(ks[28], (DEPTH, D_MODEL), f32),
        "w_ple_proj": w(ks[29], (DEPTH, PLE_DIM, D_MODEL), PLE_DIM),
        "g_ple_out": gain(ks[30], D_MODEL),
    }


def reference(x, p, g_ffn1, w_ffn1_gate, w_ffn1_up, w_ffn1_down, g_mix, w_in, g_q_lat, w_q_up,
              g_kv_lat, w_kv_up, g_mla_q, g_mla_k, g_diff_q, g_diff_k, lambda_q1, lambda_k1,
              lambda_q2, lambda_k2, g_diff_sub, w_out, g_ffn2, w_ffn2_gate, w_ffn2_up, w_ffn2_down,
              g_ple_in, w_ple_gate, b_ple_gate, w_ple_proj, g_ple_out):
    S = x.shape[1]
    cos, sin = rope_tables(S, x.dtype)
    slopes = 2.0 ** (-8.0 * jnp.arange(1, DIFF_HEADS + 1, dtype=jnp.float32) / DIFF_HEADS)
    h = x
    for i in range(DEPTH):
        lambda_init = 0.8 - 0.6 * math.exp(-0.3 * i)
        h = h + 0.5 * swiglu(rms_norm(h, g_ffn1[i]), w_ffn1_gate[i], w_ffn1_up[i], w_ffn1_down[i])
        u = rms_norm(h, g_mix[i]) @ w_in[i]
        q_lat, kv_lat, k_rope, qd, kd, vd = jnp.split(u, IN_OFFSETS, axis=-1)
        o_mla = mla_mixer(q_lat, kv_lat, k_rope, g_q_lat[i], w_q_up[i], g_kv_lat[i], w_kv_up[i],
                          g_mla_q[i], g_mla_k[i], cos, sin)
        o_diff = diff_mixer(qd, kd, vd, g_diff_q[i], g_diff_k[i], lambda_q1[i], lambda_k1[i],
                            lambda_q2[i], lambda_k2[i], g_diff_sub[i], slopes, lambda_init)
        h = h + jnp.concatenate([o_mla, o_diff], axis=-1) @ w_out[i]
        h = h + 0.5 * swiglu(rms_norm(h, g_ffn2[i]), w_ffn2_gate[i], w_ffn2_up[i], w_ffn2_down[i])
        gate = jax.nn.sigmoid(rms_norm(h, g_ple_in[i]) @ w_ple_gate[i] + b_ple_gate[i])
        h = h + gate * rms_norm(p[i] @ w_ple_proj[i], g_ple_out[i])
    return h
```

```python
import functools
import math

import jax
import jax.numpy as jnp
from jax import lax
from jax.experimental import pallas as pl
from jax.experimental.pallas import tpu as pltpu

F32 = jnp.float32
BF16 = jnp.bfloat16

MLA_HEADS = 8
MLA_NOPE = 64
MLA_ROPE = 32
MLA_QK = MLA_NOPE + MLA_ROPE
MLA_V = 64
MLA_Q_RANK = 256
MLA_KV_RANK = 128
ROPE_THETA = 10000.0
DIFF_HEADS = 4
DIFF_QK = 64
DIFF_V = 2 * DIFF_QK
EPS = 1e-6

QK_PAD = 128
ONES_ROWS = 16
NEG_BIG = -1e30

VMEM_LIMIT_BYTES = 56 * 1024 * 1024
TOKEN_TILE = 256
MLA_Q_TILE = 512
DIFF_Q_TILE = 256
KV_CHUNK = 512


def _dot(a, b):
    return jnp.dot(a, b, preferred_element_type=F32)


def _dot_nt(a, b):
    return lax.dot_general(a, b, (((1,), (1,)), ((), ())), preferred_element_type=F32)


def _dot_tn(a, b):
    return lax.dot_general(a, b, (((0,), (0,)), ((), ())), preferred_element_type=F32)


def _rms_rows(x, g_row):
    ms = jnp.mean(x * x, axis=-1, keepdims=True)
    return x * lax.rsqrt(ms + EPS) * g_row


def _rms_cols(xt, g_col):
    ms = jnp.mean(xt * xt, axis=0, keepdims=True)
    return xt * lax.rsqrt(ms + EPS) * g_col


def _swiglu(xn, wg_ref, wu_ref, wd_ref):
    gate = _dot(xn, wg_ref[...])
    up = _dot(xn, wu_ref[...])
    act = (gate / (1.0 + jnp.exp(-gate))) * up
    return _dot(act.astype(BF16), wd_ref[...])


def _rope_cols(xt, cos, sin):
    half = MLA_ROPE // 2
    rot = jnp.concatenate([-xt[half:], xt[:half]], axis=0)
    return xt * cos + rot * sin


def _pre_kernel(x_ref, g1_ref, wg_ref, wu_ref, wd_ref, gmix_ref, win_t_ref,
                gql_ref, wqu_t_ref, gkvl_ref, wkvu_t_ref,
                gqn_ref, gqr_ref, gkn_ref, gkr_ref, gdq_ref, gdk_ref, cos_ref, sin_ref,
                h1_ref, qm_ref, km_ref, vm_ref, qd_ref, kd_ref, vd_ref):
    t = x_ref.shape[0]
    x = x_ref[...]
    h1 = x + 0.5 * _swiglu(_rms_rows(x, g1_ref[...]).astype(BF16), wg_ref, wu_ref, wd_ref)
    h1_ref[...] = h1

    hn = _rms_rows(h1, gmix_ref[...]).astype(BF16)
    ut = _dot_nt(win_t_ref[...], hn)
    o_kv = MLA_Q_RANK
    o_kr = o_kv + MLA_KV_RANK
    o_qd = o_kr + MLA_ROPE
    o_kd = o_qd + DIFF_HEADS * 2 * DIFF_QK
    o_vd = o_kd + DIFF_HEADS * 2 * DIFF_QK

    cos = cos_ref[...]
    sin = sin_ref[...]
    ones = jnp.ones((ONES_ROWS, t), BF16)
    zeros_pad = jnp.zeros((QK_PAD - MLA_QK, t), F32)

    q_scale = MLA_QK ** -0.5
    qln = _rms_cols(ut[0:o_kv], gql_ref[...]).astype(BF16)
    qt = _dot(wqu_t_ref[...], qln)
    kvn = _rms_cols(ut[o_kv:o_kr], gkvl_ref[...]).astype(BF16)
    kvt = _dot(wkvu_t_ref[...], kvn)
    k_rot = _rope_cols(_rms_cols(ut[o_kr:o_qd], gkr_ref[...]), cos, sin)
    for h in range(MLA_HEADS):
        qh = qt[h * MLA_QK:(h + 1) * MLA_QK]
        q_nope = _rms_cols(qh[:MLA_NOPE], gqn_ref[...])
        q_rot = _rope_cols(_rms_cols(qh[MLA_NOPE:], gqr_ref[...]), cos, sin)
        qm_ref[0, h, 0:MLA_NOPE, :] = (q_nope * q_scale).astype(BF16)
        qm_ref[0, h, MLA_NOPE:MLA_QK, :] = (q_rot * q_scale).astype(BF16)
        qm_ref[0, h, MLA_QK:QK_PAD, :] = zeros_pad.astype(BF16)
        kvh = kvt[h * (MLA_NOPE + MLA_V):(h + 1) * (MLA_NOPE + MLA_V)]
        k_nope = _rms_cols(kvh[:MLA_NOPE], gkn_ref[...])
        k_full = jnp.concatenate([k_nope, k_rot, zeros_pad], axis=0)
        km_ref[0, h] = k_full.T.astype(BF16)
        vm_ref[0, h, 0:MLA_V, :] = kvh[MLA_NOPE:].astype(BF16)
        vm_ref[0, h, MLA_V:MLA_V + ONES_ROWS, :] = ones

    d_scale = DIFF_QK ** -0.5
    for h in range(DIFF_HEADS):
        for j in range(2):
            r = (2 * h + j) * DIFF_QK
            qd_ref[0, h, j * DIFF_QK:(j + 1) * DIFF_QK, :] = (
                _rms_cols(ut[o_qd + r:o_qd + r + DIFF_QK], gdq_ref[...]) * d_scale).astype(BF16)
        k12 = jnp.concatenate(
            [_rms_cols(ut[o_kd + (2 * h + j) * DIFF_QK:o_kd + (2 * h + j + 1) * DIFF_QK], gdk_ref[...])
             for j in range(2)], axis=0)
        kd_ref[0, h] = k12.T.astype(BF16)
        vd_ref[0, h, 0:DIFF_V, :] = ut[o_vd + h * DIFF_V:o_vd + (h + 1) * DIFF_V].astype(BF16)
        vd_ref[0, h, DIFF_V:DIFF_V + ONES_ROWS, :] = ones


def _online_softmax_sweep(qt, k_ref, v_ref, bias_fn, n_chunks):
    n = qt.shape[1]
    rows = v_ref.shape[0]

    def body(c, carry):
        m, acc = carry
        start = pl.multiple_of(c * KV_CHUNK, KV_CHUNK)
        st = _dot(k_ref[pl.ds(start, KV_CHUNK), :], qt)
        if bias_fn is not None:
            st = st + bias_fn(start)
        m_new = jnp.maximum(m, jnp.max(st, axis=0, keepdims=True))
        p = jnp.exp(st - m_new).astype(BF16)
        acc = acc * jnp.exp(m - m_new) + _dot(v_ref[:, pl.ds(start, KV_CHUNK)], p)
        return m_new, acc

    init = (jnp.full((1, n), NEG_BIG, F32), jnp.zeros((rows, n), F32))
    _, acc = lax.fori_loop(0, n_chunks, body, init)
    return acc


def _mla_attn_kernel(q_ref, k_ref, v_ref, o_ref):
    n_chunks = k_ref.shape[2] // KV_CHUNK
    acc = _online_softmax_sweep(q_ref[0, 0], k_ref.at[0, 0], v_ref.at[0, 0], None, n_chunks)
    o_ref[0, 0] = (acc[:MLA_V] / acc[MLA_V:MLA_V + 1]).astype(BF16)


def _diff_attn_kernel(slopes_ref, q_ref, k_ref, v_ref, lq1_ref, lk1_ref, lq2_ref, lk2_ref,
                      gsub_ref, o_ref, *, lambda_init):
    tq = q_ref.shape[3]
    n_chunks = k_ref.shape[2] // KV_CHUNK
    slope = slopes_ref[pl.program_id(1)]
    q0 = pl.program_id(2) * tq

    q12 = q_ref[0, 0]
    zero = jnp.zeros((DIFF_QK, tq), BF16)
    qt = jnp.concatenate([jnp.concatenate([q12[:DIFF_QK], zero], axis=1),
                          jnp.concatenate([zero, q12[DIFF_QK:]], axis=1)], axis=0)

    key_minus_query = (lax.broadcasted_iota(jnp.int32, (KV_CHUNK, tq), 0)
                       - lax.broadcasted_iota(jnp.int32, (KV_CHUNK, tq), 1))

    def bias_fn(start):
        dist = jnp.abs(key_minus_query + (start - q0)).astype(F32)
        b = -slope * dist
        return jnp.concatenate([b, b], axis=1)

    acc = _online_softmax_sweep(qt, k_ref.at[0, 0], v_ref.at[0, 0], bias_fn, n_chunks)
    o_all = acc[:DIFF_V] / acc[DIFF_V:DIFF_V + 1]
    lam = (jnp.exp(jnp.sum(lq1_ref[...] * lk1_ref[...], axis=-1, keepdims=True))
           - jnp.exp(jnp.sum(lq2_ref[...] * lk2_ref[...], axis=-1, keepdims=True)) + lambda_init)
    o = o_all[:, :tq] - lam * o_all[:, tq:]
    o_ref[0, 0] = (_rms_cols(o, gsub_ref[...]) * (1.0 - lambda_init)).astype(BF16)


def _post_kernel(h1_ref, om_ref, od_ref, p_ref, wom_ref, wod_ref, g2_ref, wg_ref, wu_ref, wd_ref,
                 gpi_ref, wpg_ref, bpg_ref, wpp_ref, gpo_ref, out_ref):
    h = h1_ref[...] + _dot_tn(om_ref[0], wom_ref[...]) + _dot_tn(od_ref[0], wod_ref[...])
    h = h + 0.5 * _swiglu(_rms_rows(h, g2_ref[...]).astype(BF16), wg_ref, wu_ref, wd_ref)
    z = _dot(_rms_rows(h, gpi_ref[...]).astype(BF16), wpg_ref[...]) + bpg_ref[...]
    gate = 1.0 / (1.0 + jnp.exp(-z))
    emb = _rms_rows(_dot(p_ref[...].astype(BF16), wpp_ref[...]), gpo_ref[...])
    out_ref[...] = h + gate * emb


def _resident(shape):
    nd = len(shape)
    return pl.BlockSpec(shape, lambda *_: (0,) * nd, pipeline_mode=pl.Buffered(1))


def _params(n_axes):
    return pltpu.CompilerParams(dimension_semantics=("arbitrary",) * n_axes,
                                vmem_limit_bytes=VMEM_LIMIT_BYTES)


def _rope_tables_t(seq):
    pos = jnp.arange(seq, dtype=F32)
    inv = ROPE_THETA ** (-jnp.arange(0, MLA_ROPE, 2, dtype=F32) / MLA_ROPE)
    ang = pos[:, None] * inv[None, :]
    ang = jnp.concatenate([ang, ang], axis=-1)
    return jnp.cos(ang).T, jnp.sin(ang).T


def kernel(x, p, g_ffn1, w_ffn1_gate, w_ffn1_up, w_ffn1_down, g_mix, w_in, g_q_lat, w_q_up, g_kv_lat, w_kv_up, g_mla_q, g_mla_k, g_diff_q, g_diff_k, lambda_q1, lambda_k1, lambda_q2, lambda_k2, g_diff_sub, w_out, g_ffn2, w_ffn2_gate, w_ffn2_up, w_ffn2_down, g_ple_in, w_ple_gate, b_ple_gate, w_ple_proj, g_ple_out):
    batch, seq, d_model = x.shape
    depth = p.shape[0]
    tokens = batch * seq
    tm = TOKEN_TILE
    tiles_per_seq = seq // tm
    assert seq % tm == 0 and seq % KV_CHUNK == 0 and seq % MLA_Q_TILE == 0 and seq % DIFF_Q_TILE == 0

    cos_t, sin_t = _rope_tables_t(seq)
    slopes = 2.0 ** (-8.0 * jnp.arange(1, DIFF_HEADS + 1, dtype=F32) / DIFF_HEADS)
    row = lambda v: v.reshape(1, -1)
    col = lambda v: v.reshape(-1, 1)
    bf = lambda w: w.astype(BF16)

    h = x.reshape(tokens, d_model)
    for i in range(depth):
        lambda_init = 0.8 - 0.6 * math.exp(-0.3 * i)
        d_ff = w_ffn1_gate.shape[-1]
        n_in = w_in.shape[-1]
        ple_dim = p.shape[-1]

        tok_spec = pl.BlockSpec((tm, d_model), lambda t: (t, 0))
        head_t = lambda heads, rows: pl.BlockSpec(
            (1, heads, rows, tm), lambda t: (t // tiles_per_seq, 0, 0, t % tiles_per_seq))
        head_r = lambda heads: pl.BlockSpec(
            (1, heads, tm, QK_PAD), lambda t: (t // tiles_per_seq, 0, t % tiles_per_seq, 0))
        rope_spec = pl.BlockSpec((MLA_ROPE, tm), lambda t: (0, t % tiles_per_seq))
        pre_inputs = [
            (h, tok_spec),
            (row(g_ffn1[i]), None), (bf(w_ffn1_gate[i]), None), (bf(w_ffn1_up[i]), None),
            (bf(w_ffn1_down[i]), None), (row(g_mix[i]), None), (bf(w_in[i].T), None),
            (col(g_q_lat[i]), None), (bf(w_q_up[i].T), None),
            (col(g_kv_lat[i]), None), (bf(w_kv_up[i].T), None),
            (col(g_mla_q[i, :MLA_NOPE]), None), (col(g_mla_q[i, MLA_NOPE:]), None),
            (col(g_mla_k[i, :MLA_NOPE]), None), (col(g_mla_k[i, MLA_NOPE:]), None),
            (col(g_diff_q[i]), None), (col(g_diff_k[i]), None),
            (cos_t, rope_spec), (sin_t, rope_spec),
        ]
        h1, qm, km, vm, qd, kd, vd = pl.pallas_call(
            _pre_kernel,
            grid=(tokens // tm,),
            in_specs=[s if s is not None else _resident(a.shape) for a, s in pre_inputs],
            out_specs=[tok_spec, head_t(MLA_HEADS, QK_PAD), head_r(MLA_HEADS),
                       head_t(MLA_HEADS, MLA_V + ONES_ROWS), head_t(DIFF_HEADS, 2 * DIFF_QK),
                       head_r(DIFF_HEADS), head_t(DIFF_HEADS, DIFF_V + ONES_ROWS)],
            out_shape=[
                jax.ShapeDtypeStruct((tokens, d_model), F32),
                jax.ShapeDtypeStruct((batch, MLA_HEADS, QK_PAD, seq), BF16),
                jax.ShapeDtypeStruct((batch, MLA_HEADS, seq, QK_PAD), BF16),
                jax.ShapeDtypeStruct((batch, MLA_HEADS, MLA_V + ONES_ROWS, seq), BF16),
                jax.ShapeDtypeStruct((batch, DIFF_HEADS, 2 * DIFF_QK, seq), BF16),
                jax.ShapeDtypeStruct((batch, DIFF_HEADS, seq, QK_PAD), BF16),
                jax.ShapeDtypeStruct((batch, DIFF_HEADS, DIFF_V + ONES_ROWS, seq), BF16),
            ],
            compiler_params=_params(1),
            name="pre",
        )(*[a for a, _ in pre_inputs])

        def attn_specs(tq, v_rows):
            return [pl.BlockSpec((1, 1, QK_PAD, tq), lambda b, hh, qi: (b, hh, 0, qi)),
                    pl.BlockSpec((1, 1, seq, QK_PAD), lambda b, hh, qi: (b, hh, 0, 0)),
                    pl.BlockSpec((1, 1, v_rows, seq), lambda b, hh, qi: (b, hh, 0, 0))]

        tq = MLA_Q_TILE
        om = pl.pallas_call(
            _mla_attn_kernel,
            grid=(batch, MLA_HEADS, seq // tq),
            in_specs=attn_specs(tq, MLA_V + ONES_ROWS),
            out_specs=pl.BlockSpec((1, 1, MLA_V, tq), lambda b, hh, qi: (b, hh, 0, qi)),
            out_shape=jax.ShapeDtypeStruct((batch, MLA_HEADS, MLA_V, seq), BF16),
            compiler_params=_params(3),
            name="mla_attn",
        )(qm, km, vm)

        tq = DIFF_Q_TILE
        small = lambda shape: pl.BlockSpec(shape, lambda b, hh, qi: (0,) * len(shape))
        od = pl.pallas_call(
            functools.partial(_diff_attn_kernel, lambda_init=lambda_init),
            grid=(batch, DIFF_HEADS, seq // tq),
            in_specs=[pl.BlockSpec(memory_space=pltpu.SMEM)] + attn_specs(tq, DIFF_V + ONES_ROWS)
                     + [small((1, DIFF_QK))] * 4 + [small((DIFF_V, 1))],
            out_specs=pl.BlockSpec((1, 1, DIFF_V, tq), lambda b, hh, qi: (b, hh, 0, qi)),
            out_shape=jax.ShapeDtypeStruct((batch, DIFF_HEADS, DIFF_V, seq), BF16),
            compiler_params=_params(3),
            name="diff_attn",
        )(slopes, qd, kd, vd, row(lambda_q1[i]), row(lambda_k1[i]), row(lambda_q2[i]),
          row(lambda_k2[i]), col(g_diff_sub[i]))

        mla_w = MLA_HEADS * MLA_V
        diff_w = DIFF_HEADS * DIFF_V
        feat_t = lambda width: pl.BlockSpec(
            (1, width, tm), lambda t: (t // tiles_per_seq, 0, t % tiles_per_seq))
        post_inputs = [
            (h1, tok_spec),
            (om.reshape(batch, mla_w, seq), feat_t(mla_w)),
            (od.reshape(batch, diff_w, seq), feat_t(diff_w)),
            (p[i].reshape(tokens, ple_dim), pl.BlockSpec((tm, ple_dim), lambda t: (t, 0))),
            (bf(w_out[i, :mla_w]), None), (bf(w_out[i, mla_w:]), None),
            (row(g_ffn2[i]), None), (bf(w_ffn2_gate[i]), None), (bf(w_ffn2_up[i]), None),
            (bf(w_ffn2_down[i]), None),
            (row(g_ple_in[i]), None), (bf(w_ple_gate[i]), None), (row(b_ple_gate[i]), None),
            (bf(w_ple_proj[i]), None), (row(g_ple_out[i]), None),
        ]
        h = pl.pallas_call(
            _post_kernel,
            grid=(tokens // tm,),
            in_specs=[s if s is not None else _resident(a.shape) for a, s in post_inputs],
            out_specs=tok_spec,
            out_shape=jax.ShapeDtypeStruct((tokens, d_model), F32),
            compiler_params=_params(1),
            name="post",
        )(*[a for a, _ in post_inputs])

    return h.reshape(batch, seq, d_model)
```

```python
import functools
import math

import jax
import jax.numpy as jnp
from jax import lax
from jax.experimental import pallas as pl
from jax.experimental.pallas import tpu as pltpu

F32 = jnp.float32
BF16 = jnp.bfloat16

MLA_HEADS = 8
MLA_NOPE = 64
MLA_ROPE = 32
MLA_QK = MLA_NOPE + MLA_ROPE
MLA_V = 64
MLA_Q_RANK = 256
MLA_KV_RANK = 128
ROPE_THETA = 10000.0
DIFF_HEADS = 4
DIFF_QK = 64
DIFF_V = 2 * DIFF_QK
EPS = 1e-6

QK_PAD = 128
ONES_ROWS = 16
NEG_BIG = -1e30

VMEM_LIMIT_BYTES = 56 * 1024 * 1024
TOKEN_TILE = 256
MLA_Q_TILE = 512
DIFF_Q_TILE = 256
KV_CHUNK = 512
LOG2E = math.log2(math.e)


def _dot(a, b):
    return jnp.dot(a, b, preferred_element_type=F32)


def _dot_nt(a, b):
    return lax.dot_general(a, b, (((1,), (1,)), ((), ())), preferred_element_type=F32)


def _dot_tn(a, b):
    return lax.dot_general(a, b, (((0,), (0,)), ((), ())), preferred_element_type=F32)


def _rms_rows(x, g_row):
    ms = jnp.mean(x * x, axis=-1, keepdims=True)
    return x * lax.rsqrt(ms + EPS) * g_row


def _rms_cols(xt, g_col):
    ms = jnp.mean(xt * xt, axis=0, keepdims=True)
    return xt * lax.rsqrt(ms + EPS) * g_col


def _swiglu(xn, wg_ref, wu_ref, wd_ref):
    gate = _dot(xn, wg_ref[...])
    up = _dot(xn, wu_ref[...])
    act = (gate / (1.0 + jnp.exp(-gate))) * up
    return _dot(act.astype(BF16), wd_ref[...])


def _rope_cols(xt, cos, sin):
    half = MLA_ROPE // 2
    rot = jnp.concatenate([-xt[half:], xt[:half]], axis=0)
    return xt * cos + rot * sin


def _pre_kernel(x_ref, g1_ref, wg_ref, wu_ref, wd_ref, gmix_ref, win_t_ref,
                gql_ref, wqu_t_ref, gkvl_ref, wkvu_t_ref,
                gqn_ref, gqr_ref, gkn_ref, gkr_ref, gdq_ref, gdk_ref, cos_ref, sin_ref,
                h1_ref, qm_ref, km_ref, vm_ref, qd_ref, kd_ref, vd_ref):
    t = x_ref.shape[0]
    x = x_ref[...]
    h1 = x + 0.5 * _swiglu(_rms_rows(x, g1_ref[...]).astype(BF16), wg_ref, wu_ref, wd_ref)
    h1_ref[...] = h1

    hn = _rms_rows(h1, gmix_ref[...]).astype(BF16)
    ut = _dot_nt(win_t_ref[...], hn)
    o_kv = MLA_Q_RANK
    o_kr = o_kv + MLA_KV_RANK
    o_qd = o_kr + MLA_ROPE
    o_kd = o_qd + DIFF_HEADS * 2 * DIFF_QK
    o_vd = o_kd + DIFF_HEADS * 2 * DIFF_QK

    cos = cos_ref[...]
    sin = sin_ref[...]
    ones = jnp.ones((ONES_ROWS, t), BF16)
    zeros_pad = jnp.zeros((QK_PAD - MLA_QK, t), F32)

    q_scale = MLA_QK ** -0.5 * LOG2E
    qln = _rms_cols(ut[0:o_kv], gql_ref[...]).astype(BF16)
    qt = _dot(wqu_t_ref[...], qln)
    kvn = _rms_cols(ut[o_kv:o_kr], gkvl_ref[...]).astype(BF16)
    kvt = _dot(wkvu_t_ref[...], kvn)
    k_rot = _rope_cols(_rms_cols(ut[o_kr:o_qd], gkr_ref[...]), cos, sin)
    for h in range(MLA_HEADS):
        qh = qt[h * MLA_QK:(h + 1) * MLA_QK]
        q_nope = _rms_cols(qh[:MLA_NOPE], gqn_ref[...])
        q_rot = _rope_cols(_rms_cols(qh[MLA_NOPE:], gqr_ref[...]), cos, sin)
        qm_ref[0, h, 0:MLA_NOPE, :] = (q_nope * q_scale).astype(BF16)
        qm_ref[0, h, MLA_NOPE:MLA_QK, :] = (q_rot * q_scale).astype(BF16)
        qm_ref[0, h, MLA_QK:QK_PAD, :] = zeros_pad.astype(BF16)
        kvh = kvt[h * (MLA_NOPE + MLA_V):(h + 1) * (MLA_NOPE + MLA_V)]
        k_nope = _rms_cols(kvh[:MLA_NOPE], gkn_ref[...])
        k_full = jnp.concatenate([k_nope, k_rot, zeros_pad], axis=0)
        km_ref[0, h] = k_full.T.astype(BF16)
        vm_ref[0, h, 0:MLA_V, :] = kvh[MLA_NOPE:].astype(BF16)
        vm_ref[0, h, MLA_V:MLA_V + ONES_ROWS, :] = ones

    d_scale = DIFF_QK ** -0.5 * LOG2E
    for h in range(DIFF_HEADS):
        for j in range(2):
            r = (2 * h + j) * DIFF_QK
            qd_ref[0, h, j * DIFF_QK:(j + 1) * DIFF_QK, :] = (
                _rms_cols(ut[o_qd + r:o_qd + r + DIFF_QK], gdq_ref[...]) * d_scale).astype(BF16)
        k12 = jnp.concatenate(
            [_rms_cols(ut[o_kd + (2 * h + j) * DIFF_QK:o_kd + (2 * h + j + 1) * DIFF_QK], gdk_ref[...])
             for j in range(2)], axis=0)
        kd_ref[0, h] = k12.T.astype(BF16)
        vd_ref[0, h, 0:DIFF_V, :] = ut[o_vd + h * DIFF_V:o_vd + (h + 1) * DIFF_V].astype(BF16)
        vd_ref[0, h, DIFF_V:DIFF_V + ONES_ROWS, :] = ones


def _online_softmax_sweep(qt, k_ref, v_ref, s_buf, bias_fn, n_chunks):
    n = qt.shape[1]
    rows = v_ref.shape[0]

    def chunk_start(c):
        return pl.multiple_of(c * KV_CHUNK, KV_CHUNK)

    def scores(c, slot):
        start = chunk_start(c)
        st = _dot(k_ref[pl.ds(start, KV_CHUNK), :], qt)
        if bias_fn is not None:
            st = st + bias_fn(start)
        s_buf[slot] = st

    def update(c, slot, carry):
        m, acc = carry
        st = s_buf[slot]
        m_new = jnp.maximum(m, jnp.max(st, axis=0, keepdims=True))
        p = jnp.exp2(st - m_new).astype(BF16)
        acc = acc * jnp.exp2(m - m_new) + _dot(v_ref[:, pl.ds(chunk_start(c), KV_CHUNK)], p)
        return m_new, acc

    def pair(i, carry):
        c = 2 * i
        scores(c + 1, 1)
        carry = update(c, 0, carry)
        scores(c + 2, 0)
        return update(c + 1, 1, carry)

    scores(0, 0)
    carry = (jnp.full((1, n), NEG_BIG, F32), jnp.zeros((rows, n), F32))
    carry = lax.fori_loop(0, n_chunks // 2 - 1, pair, carry)
    scores(n_chunks - 1, 1)
    carry = update(n_chunks - 2, 0, carry)
    _, acc = update(n_chunks - 1, 1, carry)
    return acc


def _mla_attn_kernel(q_ref, k_ref, v_ref, o_ref, s_buf):
    n_chunks = k_ref.shape[2] // KV_CHUNK
    acc = _online_softmax_sweep(q_ref[0, 0], k_ref.at[0, 0], v_ref.at[0, 0], s_buf, None, n_chunks)
    o_ref[0, 0] = (acc[:MLA_V] / acc[MLA_V:MLA_V + 1]).astype(BF16)


def _diff_attn_kernel(slopes_ref, q_ref, k_ref, v_ref, lq1_ref, lk1_ref, lq2_ref, lk2_ref,
                      gsub_ref, o_ref, s_buf, *, lambda_init):
    tq = q_ref.shape[3]
    n_chunks = k_ref.shape[2] // KV_CHUNK
    slope = slopes_ref[pl.program_id(1)]
    q0 = pl.program_id(2) * tq

    q12 = q_ref[0, 0]
    zero = jnp.zeros((DIFF_QK, tq), BF16)
    qt = jnp.concatenate([jnp.concatenate([q12[:DIFF_QK], zero], axis=1),
                          jnp.concatenate([zero, q12[DIFF_QK:]], axis=1)], axis=0)

    key_minus_query = (lax.broadcasted_iota(jnp.int32, (KV_CHUNK, tq), 0)
                       - lax.broadcasted_iota(jnp.int32, (KV_CHUNK, tq), 1))

    def bias_fn(start):
        dist = jnp.abs(key_minus_query + (start - q0)).astype(F32)
        b = (-LOG2E * slope) * dist
        return jnp.concatenate([b, b], axis=1)

    acc = _online_softmax_sweep(qt, k_ref.at[0, 0], v_ref.at[0, 0], s_buf, bias_fn, n_chunks)
    o_all = acc[:DIFF_V] / acc[DIFF_V:DIFF_V + 1]
    lam = (jnp.exp(jnp.sum(lq1_ref[...] * lk1_ref[...], axis=-1, keepdims=True))
           - jnp.exp(jnp.sum(lq2_ref[...] * lk2_ref[...], axis=-1, keepdims=True)) + lambda_init)
    o = o_all[:, :tq] - lam * o_all[:, tq:]
    o_ref[0, 0] = (_rms_cols(o, gsub_ref[...]) * (1.0 - lambda_init)).astype(BF16)


def _post_kernel(h1_ref, om_ref, od_ref, p_ref, wom_ref, wod_ref, g2_ref, wg_ref, wu_ref, wd_ref,
                 gpi_ref, wpg_ref, bpg_ref, wpp_ref, gpo_ref, out_ref):
    h = h1_ref[...] + _dot_tn(om_ref[0], wom_ref[...]) + _dot_tn(od_ref[0], wod_ref[...])
    h = h + 0.5 * _swiglu(_rms_rows(h, g2_ref[...]).astype(BF16), wg_ref, wu_ref, wd_ref)
    z = _dot(_rms_rows(h, gpi_ref[...]).astype(BF16), wpg_ref[...]) + bpg_ref[...]
    gate = 1.0 / (1.0 + jnp.exp(-z))
    emb = _rms_rows(_dot(p_ref[...].astype(BF16), wpp_ref[...]), gpo_ref[...])
    out_ref[...] = h + gate * emb


def _resident(shape):
    nd = len(shape)
    return pl.BlockSpec(shape, lambda *_: (0,) * nd, pipeline_mode=pl.Buffered(1))


def _params(n_axes):
    return pltpu.CompilerParams(dimension_semantics=("arbitrary",) * n_axes,
                                vmem_limit_bytes=VMEM_LIMIT_BYTES)


def _rope_tables_t(seq):
    pos = jnp.arange(seq, dtype=F32)
    inv = ROPE_THETA ** (-jnp.arange(0, MLA_ROPE, 2, dtype=F32) / MLA_ROPE)
    ang = pos[:, None] * inv[None, :]
    ang = jnp.concatenate([ang, ang], axis=-1)
    return jnp.cos(ang).T, jnp.sin(ang).T


def kernel(x, p, g_ffn1, w_ffn1_gate, w_ffn1_up, w_ffn1_down, g_mix, w_in, g_q_lat, w_q_up, g_kv_lat, w_kv_up, g_mla_q, g_mla_k, g_diff_q, g_diff_k, lambda_q1, lambda_k1, lambda_q2, lambda_k2, g_diff_sub, w_out, g_ffn2, w_ffn2_gate, w_ffn2_up, w_ffn2_down, g_ple_in, w_ple_gate, b_ple_gate, w_ple_proj, g_ple_out):
    batch, seq, d_model = x.shape
    depth = p.shape[0]
    tokens = batch * seq
    tm = TOKEN_TILE
    tiles_per_seq = seq // tm
    assert seq % tm == 0 and seq % (2 * KV_CHUNK) == 0
    assert seq % MLA_Q_TILE == 0 and seq % DIFF_Q_TILE == 0

    cos_t, sin_t = _rope_tables_t(seq)
    slopes = 2.0 ** (-8.0 * jnp.arange(1, DIFF_HEADS + 1, dtype=F32) / DIFF_HEADS)
    row = lambda v: v.reshape(1, -1)
    col = lambda v: v.reshape(-1, 1)
    bf = lambda w: w.astype(BF16)

    h = x.reshape(tokens, d_model)
    for i in range(depth):
        lambda_init = 0.8 - 0.6 * math.exp(-0.3 * i)
        d_ff = w_ffn1_gate.shape[-1]
        n_in = w_in.shape[-1]
        ple_dim = p.shape[-1]

        tok_spec = pl.BlockSpec((tm, d_model), lambda t: (t, 0))
        head_t = lambda heads, rows: pl.BlockSpec(
            (1, heads, rows, tm), lambda t: (t // tiles_per_seq, 0, 0, t % tiles_per_seq))
        head_r = lambda heads: pl.BlockSpec(
            (1, heads, tm, QK_PAD), lambda t: (t // tiles_per_seq, 0, t % tiles_per_seq, 0))
        rope_spec = pl.BlockSpec((MLA_ROPE, tm), lambda t: (0, t % tiles_per_seq))
        pre_inputs = [
            (h, tok_spec),
            (row(g_ffn1[i]), None), (bf(w_ffn1_gate[i]), None), (bf(w_ffn1_up[i]), None),
            (bf(w_ffn1_down[i]), None), (row(g_mix[i]), None), (bf(w_in[i].T), None),
            (col(g_q_lat[i]), None), (bf(w_q_up[i].T), None),
            (col(g_kv_lat[i]), None), (bf(w_kv_up[i].T), None),
            (col(g_mla_q[i, :MLA_NOPE]), None), (col(g_mla_q[i, MLA_NOPE:]), None),
            (col(g_mla_k[i, :MLA_NOPE]), None), (col(g_mla_k[i, MLA_NOPE:]), None),
            (col(g_diff_q[i]), None), (col(g_diff_k[i]), None),
            (cos_t, rope_spec), (sin_t, rope_spec),
        ]
        h1, qm, km, vm, qd, kd, vd = pl.pallas_call(
            _pre_kernel,
            grid=(tokens // tm,),
            in_specs=[s if s is not None else _resident(a.shape) for a, s in pre_inputs],
            out_specs=[tok_spec, head_t(MLA_HEADS, QK_PAD), head_r(MLA_HEADS),
                       head_t(MLA_HEADS, MLA_V + ONES_ROWS), head_t(DIFF_HEADS, 2 * DIFF_QK),
                       head_r(DIFF_HEADS), head_t(DIFF_HEADS, DIFF_V + ONES_ROWS)],
            out_shape=[
                jax.ShapeDtypeStruct((tokens, d_model), F32),
                jax.ShapeDtypeStruct((batch, MLA_HEADS, QK_PAD, seq), BF16),
                jax.ShapeDtypeStruct((batch, MLA_HEADS, seq, QK_PAD), BF16),
                jax.ShapeDtypeStruct((batch, MLA_HEADS, MLA_V + ONES_ROWS, seq), BF16),
                jax.ShapeDtypeStruct((batch, DIFF_HEADS, 2 * DIFF_QK, seq), BF16),
                jax.ShapeDtypeStruct((batch, DIFF_HEADS, seq, QK_PAD), BF16),
                jax.ShapeDtypeStruct((batch, DIFF_HEADS, DIFF_V + ONES_ROWS, seq), BF16),
            ],
            compiler_params=_params(1),
            name="pre",
        )(*[a for a, _ in pre_inputs])

        def attn_specs(tq, v_rows):
            return [pl.BlockSpec((1, 1, QK_PAD, tq), lambda b, hh, qi: (b, hh, 0, qi)),
                    pl.BlockSpec((1, 1, seq, QK_PAD), lambda b, hh, qi: (b, hh, 0, 0)),
                    pl.BlockSpec((1, 1, v_rows, seq), lambda b, hh, qi: (b, hh, 0, 0))]

        tq = MLA_Q_TILE
        om = pl.pallas_call(
            _mla_attn_kernel,
            grid=(batch, MLA_HEADS, seq // tq),
            in_specs=attn_specs(tq, MLA_V + ONES_ROWS),
            out_specs=pl.BlockSpec((1, 1, MLA_V, tq), lambda b, hh, qi: (b, hh, 0, qi)),
            out_shape=jax.ShapeDtypeStruct((batch, MLA_HEADS, MLA_V, seq), BF16),
            scratch_shapes=[pltpu.VMEM((2, KV_CHUNK, tq), F32)],
            compiler_params=_params(3),
            name="mla_attn",
        )(qm, km, vm)

        tq = DIFF_Q_TILE
        small = lambda shape: pl.BlockSpec(shape, lambda b, hh, qi: (0,) * len(shape))
        od = pl.pallas_call(
            functools.partial(_diff_attn_kernel, lambda_init=lambda_init),
            grid=(batch, DIFF_HEADS, seq // tq),
            in_specs=[pl.BlockSpec(memory_space=pltpu.SMEM)] + attn_specs(tq, DIFF_V + ONES_ROWS)
                     + [small((1, DIFF_QK))] * 4 + [small((DIFF_V, 1))],
            out_specs=pl.BlockSpec((1, 1, DIFF_V, tq), lambda b, hh, qi: (b, hh, 0, qi)),
            out_shape=jax.ShapeDtypeStruct((batch, DIFF_HEADS, DIFF_V, seq), BF16),
            scratch_shapes=[pltpu.VMEM((2, KV_CHUNK, 2 * tq), F32)],
            compiler_params=_params(3),
            name="diff_attn",
        )(slopes, qd, kd, vd, row(lambda_q1[i]), row(lambda_k1[i]), row(lambda_q2[i]),
          row(lambda_k2[i]), col(g_diff_sub[i]))

        mla_w = MLA_HEADS * MLA_V
        diff_w = DIFF_HEADS * DIFF_V
        feat_t = lambda width: pl.BlockSpec(
            (1, width, tm), lambda t: (t // tiles_per_seq, 0, t % tiles_per_seq))
        post_inputs = [
            (h1, tok_spec),
            (om.reshape(batch, mla_w, seq), feat_t(mla_w)),
            (od.reshape(batch, diff_w, seq), feat_t(diff_w)),
            (p[i].reshape(tokens, ple_dim), pl.BlockSpec((tm, ple_dim), lambda t: (t, 0))),
            (bf(w_out[i, :mla_w]), None), (bf(w_out[i, mla_w:]), None),
            (row(g_ffn2[i]), None), (bf(w_ffn2_gate[i]), None), (bf(w_ffn2_up[i]), None),
            (bf(w_ffn2_down[i]), None),
            (row(g_ple_in[i]), None), (bf(w_ple_gate[i]), None), (row(b_ple_gate[i]), None),
            (bf(w_ple_proj[i]), None), (row(g_ple_out[i]), None),
        ]
        h = pl.pallas_call(
            _post_kernel,
            grid=(tokens // tm,),
            in_specs=[s if s is not None else _resident(a.shape) for a, s in post_inputs],
            out_specs=tok_spec,
            out_shape=jax.ShapeDtypeStruct((tokens, d_model), F32),
            compiler_params=_params(1),
            name="post",
        )(*[a for a, _ in post_inputs])

    return h.reshape(batch, seq, d_model)
```

```python
import functools
import math

import jax
import jax.numpy as jnp
from jax import lax
from jax.experimental import pallas as pl
from jax.experimental.pallas import tpu as pltpu

F32 = jnp.float32
BF16 = jnp.bfloat16

MLA_HEADS = 8
MLA_NOPE = 64
MLA_ROPE = 32
MLA_QK = MLA_NOPE + MLA_ROPE
MLA_V = 64
MLA_Q_RANK = 256
MLA_KV_RANK = 128
ROPE_THETA = 10000.0
DIFF_HEADS = 4
DIFF_QK = 64
DIFF_V = 2 * DIFF_QK
EPS = 1e-6

QK_PAD = 128
ONES_ROWS = 16
AUG_ROWS = 16
DIFF_K_WIDTH = 256
POS_BLOCK = 256
NEG_BIG = -1e30
LOG2E = math.log2(math.e)

VMEM_LIMIT_BYTES = 56 * 1024 * 1024
TOKEN_TILE = 256
MLA_Q_TILE = 512
DIFF_Q_TILE = 256
KV_CHUNK = 512


def _dot(a, b):
    return jnp.dot(a, b, preferred_element_type=F32)


def _dot_nt(a, b):
    return lax.dot_general(a, b, (((1,), (1,)), ((), ())), preferred_element_type=F32)


def _dot_tn(a, b):
    return lax.dot_general(a, b, (((0,), (0,)), ((), ())), preferred_element_type=F32)


def _rms_rows(x, g_row):
    ms = jnp.mean(x * x, axis=-1, keepdims=True)
    return x * lax.rsqrt(ms + EPS) * g_row


def _rms_cols(xt, g_col):
    ms = jnp.mean(xt * xt, axis=0, keepdims=True)
    return xt * lax.rsqrt(ms + EPS) * g_col


def _swiglu(xn, wg_ref, wu_ref, wd_ref):
    gate = _dot(xn, wg_ref[...])
    up = _dot(xn, wu_ref[...])
    act = (gate / (1.0 + jnp.exp(-gate))) * up
    return _dot(act.astype(BF16), wd_ref[...])


def _rope_cols(xt, cos, sin):
    half = MLA_ROPE // 2
    rot = jnp.concatenate([-xt[half:], xt[:half]], axis=0)
    return xt * cos + rot * sin


def _pre_kernel(x_ref, g1_ref, wg_ref, wu_ref, wd_ref, gmix_ref, win_t_ref,
                gql_ref, wqu_t_ref, gkvl_ref, wkvu_t_ref,
                gqn_ref, gqr_ref, gkn_ref, gkr_ref, gdq_ref, gdk_ref, cos_ref, sin_ref,
                qaug_ref, kaug_ref,
                h1_ref, qm_ref, km_ref, vm_ref, qd_ref, kd_ref, vd_ref):
    t = x_ref.shape[0]
    x = x_ref[...]
    h1 = x + 0.5 * _swiglu(_rms_rows(x, g1_ref[...]).astype(BF16), wg_ref, wu_ref, wd_ref)
    h1_ref[...] = h1

    hn = _rms_rows(h1, gmix_ref[...]).astype(BF16)
    ut = _dot_nt(win_t_ref[...], hn)
    o_kv = MLA_Q_RANK
    o_kr = o_kv + MLA_KV_RANK
    o_qd = o_kr + MLA_ROPE
    o_kd = o_qd + DIFF_HEADS * 2 * DIFF_QK
    o_vd = o_kd + DIFF_HEADS * 2 * DIFF_QK

    cos = cos_ref[...]
    sin = sin_ref[...]
    ones = jnp.ones((ONES_ROWS, t), BF16)
    zeros_pad = jnp.zeros((QK_PAD - MLA_QK, t), F32)

    q_scale = MLA_QK ** -0.5 * LOG2E
    qln = _rms_cols(ut[0:o_kv], gql_ref[...]).astype(BF16)
    qt = _dot(wqu_t_ref[...], qln)
    kvn = _rms_cols(ut[o_kv:o_kr], gkvl_ref[...]).astype(BF16)
    kvt = _dot(wkvu_t_ref[...], kvn)
    k_rot = _rope_cols(_rms_cols(ut[o_kr:o_qd], gkr_ref[...]), cos, sin)
    for h in range(MLA_HEADS):
        qh = qt[h * MLA_QK:(h + 1) * MLA_QK]
        q_nope = _rms_cols(qh[:MLA_NOPE], gqn_ref[...])
        q_rot = _rope_cols(_rms_cols(qh[MLA_NOPE:], gqr_ref[...]), cos, sin)
        qm_ref[0, h, 0:MLA_NOPE, :] = (q_nope * q_scale).astype(BF16)
        qm_ref[0, h, MLA_NOPE:MLA_QK, :] = (q_rot * q_scale).astype(BF16)
        qm_ref[0, h, MLA_QK:QK_PAD, :] = zeros_pad.astype(BF16)
        kvh = kvt[h * (MLA_NOPE + MLA_V):(h + 1) * (MLA_NOPE + MLA_V)]
        k_nope = _rms_cols(kvh[:MLA_NOPE], gkn_ref[...])
        k_full = jnp.concatenate([k_nope, k_rot, zeros_pad], axis=0)
        km_ref[0, h] = k_full.T.astype(BF16)
        vm_ref[0, h, 0:MLA_V, :] = kvh[MLA_NOPE:].astype(BF16)
        vm_ref[0, h, MLA_V:MLA_V + ONES_ROWS, :] = ones

    d_scale = DIFF_QK ** -0.5 * LOG2E
    zero_q = jnp.zeros((DIFF_QK, t), BF16)
    for h in range(DIFF_HEADS):
        for j in range(2):
            r = o_qd + (2 * h + j) * DIFF_QK
            qj = (_rms_cols(ut[r:r + DIFF_QK], gdq_ref[...]) * d_scale).astype(BF16)
            qd_ref[0, h, j * DIFF_QK:(j + 1) * DIFF_QK, j * t:(j + 1) * t] = qj
            qd_ref[0, h, (1 - j) * DIFF_QK:(2 - j) * DIFF_QK, j * t:(j + 1) * t] = zero_q
        qd_ref[0, h, 2 * DIFF_QK:2 * DIFF_QK + AUG_ROWS, :] = qaug_ref[h]
        k12 = jnp.concatenate(
            [_rms_cols(ut[o_kd + (2 * h + j) * DIFF_QK:o_kd + (2 * h + j + 1) * DIFF_QK], gdk_ref[...])
             for j in range(2)], axis=0)
        kd_ref[0, h, :, 0:2 * DIFF_QK] = k12.T.astype(BF16)
        kd_ref[0, h, :, 2 * DIFF_QK:DIFF_K_WIDTH] = kaug_ref[h]
        vd_ref[0, h, 0:DIFF_V, :] = ut[o_vd + h * DIFF_V:o_vd + (h + 1) * DIFF_V].astype(BF16)
        vd_ref[0, h, DIFF_V:DIFF_V + ONES_ROWS, :] = ones


def _key_rows(c):
    return pl.ds(pl.multiple_of(c * KV_CHUNK, KV_CHUNK), KV_CHUNK)


def _store_scores(s_buf, max_buf, slot, scores):
    sb = scores.astype(BF16)
    s_buf[slot] = sb
    max_buf[slot] = jnp.max(sb, axis=0, keepdims=True).astype(F32)


def _sweep_head(*, n_tiles, n_chunks, width, rows, first_chunk, issue_scores, v_chunk, finalize,
                s_buf, max_buf, p_buf):
    n_pairs = n_chunks // 2
    assert n_pairs % 2 == 0 and n_pairs >= 2

    def chunk_of(qi, i, k):
        return (first_chunk(qi) + 2 * i + k) % n_chunks

    def step(qi, i, parity, carry, first=False, last=False):
        m, alphas, acc = carry
        cur = [(s_buf[parity, k], max_buf[parity, k]) for k in range(2)]

        nxt_qi, nxt_i = (qi + 1, 0) if last else (qi, i + 1)
        nxt_qi = jnp.minimum(nxt_qi, n_tiles - 1)
        for k in range(2):
            issue_scores(nxt_qi, chunk_of(nxt_qi, nxt_i, k), last and k == 0, (1 - parity, k))

        prv_qi, prv_i = (jnp.maximum(qi - 1, 0), n_pairs - 1) if first else (qi, i - 1)
        for k in range(2):
            acc = acc * alphas[k] + _dot(v_chunk(chunk_of(prv_qi, prv_i, k)), p_buf[1 - parity, k])

        if first:
            m = jnp.full((1, width), NEG_BIG, F32)
        alphas = []
        for k, (sb, chunk_max) in enumerate(cur):
            m_new = jnp.maximum(m, chunk_max)
            p_buf[parity, k] = jnp.exp2(sb - m_new.astype(BF16))
            alphas.append(jnp.exp2(m - m_new))
            m = m_new
        return m, tuple(alphas), acc

    def tile(qi, carry):
        def two_steps(j, cr):
            cr = step(qi, 2 * j + 1, 1, cr)
            return step(qi, 2 * j + 2, 0, cr)
        carry = lax.fori_loop(0, n_pairs // 2 - 1, two_steps, carry)
        carry = step(qi, n_pairs - 1, 1, carry, last=True)
        carry = step(qi + 1, 0, 0, carry, first=True)
        finalize(qi, carry[2])
        return carry

    p_buf[...] = jnp.zeros(p_buf.shape, p_buf.dtype)
    for k in range(2):
        issue_scores(0, chunk_of(0, 0, k), k == 0, (0, k))
    zero_row = jnp.zeros((1, width), F32)
    carry = (zero_row, (zero_row, zero_row), jnp.zeros((rows, width), F32))
    carry = step(0, 0, 0, carry, first=True)
    lax.fori_loop(0, n_tiles, tile, carry)


def _mla_attn_kernel(q_ref, k_ref, v_ref, o_ref, s_buf, max_buf, p_buf):
    tq = s_buf.shape[3]
    seq = k_ref.shape[2]

    def tile_cols(qi):
        return pl.ds(pl.multiple_of(qi * tq, tq), tq)

    def issue_scores(qi, c, is_first_chunk, slot):
        _store_scores(s_buf, max_buf, slot,
                      _dot(k_ref[0, 0, _key_rows(c), :], q_ref[0, 0, :, tile_cols(qi)]))

    def finalize(qi, acc):
        o_ref[0, 0, :, tile_cols(qi)] = (acc[:MLA_V] / acc[MLA_V:MLA_V + 1]).astype(BF16)

    _sweep_head(n_tiles=seq // tq, n_chunks=seq // KV_CHUNK, width=tq, rows=v_ref.shape[2],
                first_chunk=lambda qi: 0, issue_scores=issue_scores,
                v_chunk=lambda c: v_ref[0, 0, :, _key_rows(c)], finalize=finalize,
                s_buf=s_buf, max_buf=max_buf, p_buf=p_buf)


def _diff_attn_kernel(q_ref, k_ref, v_ref, lq1_ref, lk1_ref, lq2_ref, lk2_ref, gsub_ref, o_ref,
                      s_buf, max_buf, p_buf, *, lambda_init):
    tq = s_buf.shape[3] // 2
    seq = k_ref.shape[2]
    assert KV_CHUNK % tq == 0
    first_chunk = lambda qi: (qi * tq) // KV_CHUNK

    lam = (jnp.exp(jnp.sum(lq1_ref[...] * lk1_ref[...], axis=-1, keepdims=True))
           - jnp.exp(jnp.sum(lq2_ref[...] * lk2_ref[...], axis=-1, keepdims=True)) + lambda_init)
    zero_rows = jnp.zeros((DIFF_K_WIDTH - 2 * DIFF_QK - AUG_ROWS, 2 * tq), BF16)

    def issue_scores(qi, c, is_first_chunk, slot):
        qa = q_ref[0, 0, :, pl.ds(pl.multiple_of(qi * 2 * tq, 2 * tq), 2 * tq)]
        kc = k_ref[0, 0, _key_rows(c), :]

        def scores(sign):
            aug = (qa[2 * DIFF_QK:].astype(F32) * sign).astype(BF16)
            return _dot(kc, jnp.concatenate([qa[:2 * DIFF_QK], aug, zero_rows], axis=0))

        if is_first_chunk:
            st = jnp.minimum(scores(1.0), scores(-1.0))
        else:
            st = scores(jnp.where(c < first_chunk(qi), 1.0, -1.0).astype(F32))
        _store_scores(s_buf, max_buf, slot, st)

    def finalize(qi, acc):
        o_all = acc[:DIFF_V] / acc[DIFF_V:DIFF_V + 1]
        o = o_all[:, :tq] - lam * o_all[:, tq:]
        o_ref[0, 0, :, pl.ds(pl.multiple_of(qi * tq, tq), tq)] = (
            _rms_cols(o, gsub_ref[...]) * (1.0 - lambda_init)).astype(BF16)

    _sweep_head(n_tiles=seq // tq, n_chunks=seq // KV_CHUNK, width=2 * tq, rows=v_ref.shape[2],
                first_chunk=first_chunk, issue_scores=issue_scores,
                v_chunk=lambda c: v_ref[0, 0, :, _key_rows(c)], finalize=finalize,
                s_buf=s_buf, max_buf=max_buf, p_buf=p_buf)


def _post_kernel(h1_ref, om_ref, od_ref, p_ref, wom_ref, wod_ref, g2_ref, wg_ref, wu_ref, wd_ref,
                 gpi_ref, wpg_ref, bpg_ref, wpp_ref, gpo_ref, out_ref):
    h = h1_ref[...] + _dot_tn(om_ref[0], wom_ref[...]) + _dot_tn(od_ref[0], wod_ref[...])
    h = h + 0.5 * _swiglu(_rms_rows(h, g2_ref[...]).astype(BF16), wg_ref, wu_ref, wd_ref)
    z = _dot(_rms_rows(h, gpi_ref[...]).astype(BF16), wpg_ref[...]) + bpg_ref[...]
    gate = 1.0 / (1.0 + jnp.exp(-z))
    emb = _rms_rows(_dot(p_ref[...].astype(BF16), wpp_ref[...]), gpo_ref[...])
    out_ref[...] = h + gate * emb


def _resident(shape):
    nd = len(shape)
    return pl.BlockSpec(shape, lambda *_: (0,) * nd, pipeline_mode=pl.Buffered(1))


def _params(n_axes):
    return pltpu.CompilerParams(dimension_semantics=("arbitrary",) * n_axes,
                                vmem_limit_bytes=VMEM_LIMIT_BYTES)


def _attn_scratch(width):
    return [pltpu.VMEM((2, 2, KV_CHUNK, width), BF16), pltpu.VMEM((2, 2, 1, width), F32),
            pltpu.VMEM((2, 2, KV_CHUNK, width), BF16)]


def _rope_tables_t(seq):
    pos = jnp.arange(seq, dtype=F32)
    inv = ROPE_THETA ** (-jnp.arange(0, MLA_ROPE, 2, dtype=F32) / MLA_ROPE)
    ang = pos[:, None] * inv[None, :]
    ang = jnp.concatenate([ang, ang], axis=-1)
    return jnp.cos(ang).T, jnp.sin(ang).T


def _alibi_operands(slopes2, seq, tq):
    heads = slopes2.shape[0]
    hi = slopes2.astype(BF16).astype(F32)
    lo = slopes2 - hi
    pos = jnp.arange(seq, dtype=jnp.int32)
    blk = (pos // POS_BLOCK).astype(F32)
    rem = (pos % POS_BLOCK).astype(F32)
    per_head = lambda v: jnp.broadcast_to(v[:, None], (heads, seq))
    per_pos = lambda v: jnp.broadcast_to(v[None, :], (heads, seq))
    q_rows = [per_pos(blk), per_pos(blk), per_pos(rem), per_pos(rem),
              per_head(POS_BLOCK * hi), per_head(POS_BLOCK * lo), per_head(hi), per_head(lo)]
    k_cols = [per_head(-POS_BLOCK * hi), per_head(-POS_BLOCK * lo), per_head(-hi), per_head(-lo),
              per_pos(blk), per_pos(blk), per_pos(rem), per_pos(rem)]
    q_aug = jnp.stack(q_rows, axis=1)
    q_aug = jnp.pad(q_aug, ((0, 0), (0, AUG_ROWS - len(q_rows)), (0, 0)))
    q_aug = q_aug.reshape(heads, AUG_ROWS, seq // tq, 1, tq)
    q_aug = jnp.broadcast_to(q_aug, (heads, AUG_ROWS, seq // tq, 2, tq)).reshape(heads, AUG_ROWS, 2 * seq)
    k_aug = jnp.stack(k_cols, axis=2)
    k_aug = jnp.pad(k_aug, ((0, 0), (0, 0), (0, DIFF_K_WIDTH - 2 * DIFF_QK - len(k_cols))))
    return q_aug.astype(BF16), k_aug.astype(BF16)


def kernel(x, p, g_ffn1, w_ffn1_gate, w_ffn1_up, w_ffn1_down, g_mix, w_in, g_q_lat, w_q_up, g_kv_lat, w_kv_up, g_mla_q, g_mla_k, g_diff_q, g_diff_k, lambda_q1, lambda_k1, lambda_q2, lambda_k2, g_diff_sub, w_out, g_ffn2, w_ffn2_gate, w_ffn2_up, w_ffn2_down, g_ple_in, w_ple_gate, b_ple_gate, w_ple_proj, g_ple_out):
    batch, seq, d_model = x.shape
    depth = p.shape[0]
    tokens = batch * seq
    tm = TOKEN_TILE
    tiles_per_seq = seq // tm
    assert seq % tm == 0 and seq % (4 * KV_CHUNK) == 0 and seq % MLA_Q_TILE == 0
    assert tm == DIFF_Q_TILE == POS_BLOCK

    cos_t, sin_t = _rope_tables_t(seq)
    slopes2 = LOG2E * 2.0 ** (-8.0 * jnp.arange(1, DIFF_HEADS + 1, dtype=F32) / DIFF_HEADS)
    q_aug, k_aug = _alibi_operands(slopes2, seq, DIFF_Q_TILE)
    row = lambda v: v.reshape(1, -1)
    col = lambda v: v.reshape(-1, 1)
    bf = lambda w: w.astype(BF16)

    h = x.reshape(tokens, d_model)
    for i in range(depth):
        lambda_init = 0.8 - 0.6 * math.exp(-0.3 * i)
        ple_dim = p.shape[-1]

        tok_spec = pl.BlockSpec((tm, d_model), lambda t: (t, 0))
        head_t = lambda heads, rows, width: pl.BlockSpec(
            (1, heads, rows, width), lambda t: (t // tiles_per_seq, 0, 0, t % tiles_per_seq))
        head_r = lambda heads, width: pl.BlockSpec(
            (1, heads, tm, width), lambda t: (t // tiles_per_seq, 0, t % tiles_per_seq, 0))
        rope_spec = pl.BlockSpec((MLA_ROPE, tm), lambda t: (0, t % tiles_per_seq))
        pre_inputs = [
            (h, tok_spec),
            (row(g_ffn1[i]), None), (bf(w_ffn1_gate[i]), None), (bf(w_ffn1_up[i]), None),
            (bf(w_ffn1_down[i]), None), (row(g_mix[i]), None), (bf(w_in[i].T), None),
            (col(g_q_lat[i]), None), (bf(w_q_up[i].T), None),
            (col(g_kv_lat[i]), None), (bf(w_kv_up[i].T), None),
            (col(g_mla_q[i, :MLA_NOPE]), None), (col(g_mla_q[i, MLA_NOPE:]), None),
            (col(g_mla_k[i, :MLA_NOPE]), None), (col(g_mla_k[i, MLA_NOPE:]), None),
            (col(g_diff_q[i]), None), (col(g_diff_k[i]), None),
            (cos_t, rope_spec), (sin_t, rope_spec),
            (q_aug, pl.BlockSpec((DIFF_HEADS, AUG_ROWS, 2 * tm), lambda t: (0, 0, t % tiles_per_seq))),
            (k_aug, pl.BlockSpec((DIFF_HEADS, tm, DIFF_K_WIDTH - 2 * DIFF_QK),
                                 lambda t: (0, t % tiles_per_seq, 0))),
        ]
        dq_rows = 2 * DIFF_QK + AUG_ROWS
        h1, qm, km, vm, qd, kd, vd = pl.pallas_call(
            _pre_kernel,
            grid=(tokens // tm,),
            in_specs=[s if s is not None else _resident(a.shape) for a, s in pre_inputs],
            out_specs=[tok_spec, head_t(MLA_HEADS, QK_PAD, tm), head_r(MLA_HEADS, QK_PAD),
                       head_t(MLA_HEADS, MLA_V + ONES_ROWS, tm), head_t(DIFF_HEADS, dq_rows, 2 * tm),
                       head_r(DIFF_HEADS, DIFF_K_WIDTH), head_t(DIFF_HEADS, DIFF_V + ONES_ROWS, tm)],
            out_shape=[
                jax.ShapeDtypeStruct((tokens, d_model), F32),
                jax.ShapeDtypeStruct((batch, MLA_HEADS, QK_PAD, seq), BF16),
                jax.ShapeDtypeStruct((batch, MLA_HEADS, seq, QK_PAD), BF16),
                jax.ShapeDtypeStruct((batch, MLA_HEADS, MLA_V + ONES_ROWS, seq), BF16),
                jax.ShapeDtypeStruct((batch, DIFF_HEADS, dq_rows, 2 * seq), BF16),
                jax.ShapeDtypeStruct((batch, DIFF_HEADS, seq, DIFF_K_WIDTH), BF16),
                jax.ShapeDtypeStruct((batch, DIFF_HEADS, DIFF_V + ONES_ROWS, seq), BF16),
            ],
            compiler_params=_params(1),
            name="pre",
        )(*[a for a, _ in pre_inputs])

        per_head = lambda *tail: pl.BlockSpec((1, 1) + tail, lambda b, hh: (b, hh, 0, 0))
        om = pl.pallas_call(
            _mla_attn_kernel,
            grid=(batch, MLA_HEADS),
            in_specs=[per_head(QK_PAD, seq), per_head(seq, QK_PAD), per_head(MLA_V + ONES_ROWS, seq)],
            out_specs=per_head(MLA_V, seq),
            out_shape=jax.ShapeDtypeStruct((batch, MLA_HEADS, MLA_V, seq), BF16),
            scratch_shapes=_attn_scratch(MLA_Q_TILE),
            compiler_params=_params(2),
            name="mla_attn",
        )(qm, km, vm)

        small = lambda shape: pl.BlockSpec(shape, lambda b, hh: (0,) * len(shape))
        od = pl.pallas_call(
            functools.partial(_diff_attn_kernel, lambda_init=lambda_init),
            grid=(batch, DIFF_HEADS),
            in_specs=[per_head(dq_rows, 2 * seq), per_head(seq, DIFF_K_WIDTH),
                      per_head(DIFF_V + ONES_ROWS, seq)]
                     + [small((1, DIFF_QK))] * 4 + [small((DIFF_V, 1))],
            out_specs=per_head(DIFF_V, seq),
            out_shape=jax.ShapeDtypeStruct((batch, DIFF_HEADS, DIFF_V, seq), BF16),
            scratch_shapes=_attn_scratch(2 * DIFF_Q_TILE),
            compiler_params=_params(2),
            name="diff_attn",
        )(qd, kd, vd, row(lambda_q1[i]), row(lambda_k1[i]), row(lambda_q2[i]),
          row(lambda_k2[i]), col(g_diff_sub[i]))

        mla_w = MLA_HEADS * MLA_V
        diff_w = DIFF_HEADS * DIFF_V
        feat_t = lambda width: pl.BlockSpec(
            (1, width, tm), lambda t: (t // tiles_per_seq, 0, t % tiles_per_seq))
        post_inputs = [
            (h1, tok_spec),
            (om.reshape(batch, mla_w, seq), feat_t(mla_w)),
            (od.reshape(batch, diff_w, seq), feat_t(diff_w)),
            (p[i].reshape(tokens, ple_dim), pl.BlockSpec((tm, ple_dim), lambda t: (t, 0))),
            (bf(w_out[i, :mla_w]), None), (bf(w_out[i, mla_w:]), None),
            (row(g_ffn2[i]), None), (bf(w_ffn2_gate[i]), None), (bf(w_ffn2_up[i]), None),
            (bf(w_ffn2_down[i]), None),
            (row(g_ple_in[i]), None), (bf(w_ple_gate[i]), None), (row(b_ple_gate[i]), None),
            (bf(w_ple_proj[i]), None), (row(g_ple_out[i]), None),
        ]
        h = pl.pallas_call(
            _post_kernel,
            grid=(tokens // tm,),
            in_specs=[s if s is not None else _resident(a.shape) for a, s in post_inputs],
            out_specs=tok_spec,
            out_shape=jax.ShapeDtypeStruct((tokens, d_model), F32),
            compiler_params=_params(1),
            name="post",
        )(*[a for a, _ in post_inputs])

    return h.reshape(batch, seq, d_model)
```

```python
import functools
import math

import jax
import jax.numpy as jnp
from jax import lax
from jax.experimental import pallas as pl
from jax.experimental.pallas import tpu as pltpu

F32 = jnp.float32
BF16 = jnp.bfloat16

MLA_HEADS = 8
MLA_NOPE = 64
MLA_ROPE = 32
MLA_QK = MLA_NOPE + MLA_ROPE
MLA_V = 64
MLA_Q_RANK = 256
MLA_KV_RANK = 128
ROPE_THETA = 10000.0
DIFF_HEADS = 4
DIFF_QK = 64
DIFF_V = 2 * DIFF_QK
EPS = 1e-6

QK_PAD = 128
ONES_ROWS = 16
AUG_ROWS = 16
DIFF_K_WIDTH = 256
POS_BLOCK = 256
NEG_BIG = -1e30
LANES = 128
LOG2E = math.log2(math.e)

VMEM_LIMIT_BYTES = 56 * 1024 * 1024
TOKEN_TILE = 256
MLA_Q_TILE = 512
DIFF_Q_TILE = 256
MLA_KV_CHUNK = 512
DIFF_KV_CHUNK = 512


def _dot(a, b):
    return jnp.dot(a, b, preferred_element_type=F32)


def _dot_nt(a, b):
    return lax.dot_general(a, b, (((1,), (1,)), ((), ())), preferred_element_type=F32)


def _dot_tn(a, b):
    return lax.dot_general(a, b, (((0,), (0,)), ((), ())), preferred_element_type=F32)


def _rms_rows(x, g_row):
    ms = jnp.mean(x * x, axis=-1, keepdims=True)
    return x * lax.rsqrt(ms + EPS) * g_row


def _rms_cols(xt, g_col):
    ms = jnp.mean(xt * xt, axis=0, keepdims=True)
    return xt * lax.rsqrt(ms + EPS) * g_col


def _swiglu(xn, wg_ref, wu_ref, wd_ref):
    gate = _dot(xn, wg_ref[...])
    up = _dot(xn, wu_ref[...])
    act = (gate / (1.0 + jnp.exp(-gate))) * up
    return _dot(act.astype(BF16), wd_ref[...])


def _rope_cols(xt, cos, sin):
    half = MLA_ROPE // 2
    rot = jnp.concatenate([-xt[half:], xt[:half]], axis=0)
    return xt * cos + rot * sin


def _pre_kernel(x_ref, g1_ref, wg_ref, wu_ref, wd_ref, gmix_ref, win_t_ref,
                gql_ref, wqu_t_ref, gkvl_ref, wkvu_t_ref,
                gqn_ref, gqr_ref, gkn_ref, gkr_ref, gdq_ref, gdk_ref, cos_ref, sin_ref,
                qaug_ref, kaug_ref,
                h1_ref, qm_ref, km_ref, vm_ref, qd_ref, kd_ref, vd_ref, norm_ref):
    t = x_ref.shape[0]
    x = x_ref[...]
    h1 = x + 0.5 * _swiglu(_rms_rows(x, g1_ref[...]).astype(BF16), wg_ref, wu_ref, wd_ref)
    h1_ref[...] = h1

    hn = _rms_rows(h1, gmix_ref[...]).astype(BF16)
    ut = _dot_nt(win_t_ref[...], hn)
    o_kv = MLA_Q_RANK
    o_kr = o_kv + MLA_KV_RANK
    o_qd = o_kr + MLA_ROPE
    o_kd = o_qd + DIFF_HEADS * 2 * DIFF_QK
    o_vd = o_kd + DIFF_HEADS * 2 * DIFF_QK

    cos = cos_ref[...]
    sin = sin_ref[...]
    ones = jnp.ones((ONES_ROWS, t), BF16)
    zeros_pad = jnp.zeros((QK_PAD - MLA_QK, t), F32)

    q_scale = MLA_QK ** -0.5 * LOG2E
    qln = _rms_cols(ut[0:o_kv], gql_ref[...]).astype(BF16)
    qt = _dot(wqu_t_ref[...], qln)
    kvn = _rms_cols(ut[o_kv:o_kr], gkvl_ref[...]).astype(BF16)
    kvt = _dot(wkvu_t_ref[...], kvn)
    k_rot = _rope_cols(_rms_cols(ut[o_kr:o_qd], gkr_ref[...]), cos, sin)
    for h in range(MLA_HEADS):
        qh = qt[h * MLA_QK:(h + 1) * MLA_QK]
        q_nope = _rms_cols(qh[:MLA_NOPE], gqn_ref[...])
        q_rot = _rope_cols(_rms_cols(qh[MLA_NOPE:], gqr_ref[...]), cos, sin)
        qm_ref[0, h, 0:MLA_NOPE, :] = (q_nope * q_scale).astype(BF16)
        qm_ref[0, h, MLA_NOPE:MLA_QK, :] = (q_rot * q_scale).astype(BF16)
        qm_ref[0, h, MLA_QK:QK_PAD, :] = zeros_pad.astype(BF16)
        kvh = kvt[h * (MLA_NOPE + MLA_V):(h + 1) * (MLA_NOPE + MLA_V)]
        k_nope = _rms_cols(kvh[:MLA_NOPE], gkn_ref[...])
        k_full = jnp.concatenate([k_nope, k_rot, zeros_pad], axis=0)
        km_ref[0, h] = k_full.T.astype(BF16)
        vm_ref[0, h, 0:MLA_V, :] = kvh[MLA_NOPE:].astype(BF16)
        vm_ref[0, h, MLA_V:MLA_V + ONES_ROWS, :] = ones

    d_scale = DIFF_QK ** -0.5 * LOG2E
    zero_q = jnp.zeros((DIFF_QK, t), BF16)
    n_maps = 2 * DIFF_HEADS

    def put_max_sq_norm(row, xt):
        sq = jnp.max(jnp.sum(xt * xt, axis=0, keepdims=True), axis=1, keepdims=True)
        norm_ref[0, row:row + 1, :] = jnp.broadcast_to(sq, (1, norm_ref.shape[2]))

    for h in range(DIFF_HEADS):
        k_maps = []
        for j in range(2):
            r = (2 * h + j) * DIFF_QK
            qj = _rms_cols(ut[o_qd + r:o_qd + r + DIFF_QK], gdq_ref[...]) * d_scale
            kj = _rms_cols(ut[o_kd + r:o_kd + r + DIFF_QK], gdk_ref[...])
            put_max_sq_norm(2 * h + j, qj)
            put_max_sq_norm(n_maps + 2 * h + j, kj)
            qd_ref[0, h, j * DIFF_QK:(j + 1) * DIFF_QK, j * t:(j + 1) * t] = qj.astype(BF16)
            qd_ref[0, h, (1 - j) * DIFF_QK:(2 - j) * DIFF_QK, j * t:(j + 1) * t] = zero_q
            k_maps.append(kj)
        qd_ref[0, h, 2 * DIFF_QK:2 * DIFF_QK + AUG_ROWS, :] = qaug_ref[h]
        k12 = jnp.concatenate(k_maps, axis=0)
        kd_ref[0, h, :, 0:2 * DIFF_QK] = k12.T.astype(BF16)
        kd_ref[0, h, :, 2 * DIFF_QK:DIFF_K_WIDTH] = kaug_ref[h]
        vd_ref[0, h, 0:DIFF_V, :] = ut[o_vd + h * DIFF_V:o_vd + (h + 1) * DIFF_V].astype(BF16)
        vd_ref[0, h, DIFF_V:DIFF_V + ONES_ROWS, :] = ones


def _key_rows(c, size):
    return pl.ds(pl.multiple_of(c * size, size), size)


def _store_scores(s_buf, max_buf, slot, scores):
    sb = scores.astype(BF16)
    s_buf[slot] = sb
    max_buf[slot] = jnp.max(sb, axis=0, keepdims=True).astype(F32)


def _sweep_head(*, n_tiles, width, window, first_chunk, issue_scores, v_chunk, finalize,
                s_buf, max_buf, p_buf, acc_buf):
    def chunk_of(qi, i, k):
        lo, count = window(qi)
        return lo + (first_chunk(qi) - lo + 2 * i + k) % count

    def step(qi, i, parity, carry, first=False, last=False):
        m, alphas = carry
        cur = [(s_buf[parity, k], max_buf[parity, k]) for k in range(2)]

        nxt_qi, nxt_i = (qi + 1, 0) if last else (qi, i + 1)
        nxt_qi = jnp.minimum(nxt_qi, n_tiles - 1)
        for k in range(2):
            issue_scores(nxt_qi, chunk_of(nxt_qi, nxt_i, k), last and k == 0, (1 - parity, k))

        if first:
            prv_qi = jnp.maximum(qi - 1, 0)
            prv_i = window(prv_qi)[1] // 2 - 1
        else:
            prv_qi, prv_i = qi, i - 1
        for k in range(2):
            acc_buf[...] = (acc_buf[...] * alphas[k]
                            + _dot(v_chunk(chunk_of(prv_qi, prv_i, k)), p_buf[1 - parity, k]))

        if first:
            m = jnp.full((1, width), NEG_BIG, F32)
        alphas = []
        for k, (sb, chunk_max) in enumerate(cur):
            m_new = jnp.maximum(m, chunk_max)
            p_buf[parity, k] = jnp.exp2(sb - m_new.astype(BF16))
            alphas.append(jnp.exp2(m - m_new))
            m = m_new
        return m, tuple(alphas)

    def tile(qi, carry):
        def two_steps(j, cr):
            cr = step(qi, 2 * j + 1, 1, cr)
            return step(qi, 2 * j + 2, 0, cr)
        n_pairs = window(qi)[1] // 2
        carry = lax.fori_loop(0, n_pairs // 2 - 1, two_steps, carry)
        carry = step(qi, n_pairs - 1, 1, carry, last=True)
        carry = step(qi + 1, 0, 0, carry, first=True)
        finalize(qi)
        return carry

    p_buf[...] = jnp.zeros(p_buf.shape, p_buf.dtype)
    acc_buf[...] = jnp.zeros(acc_buf.shape, acc_buf.dtype)
    for k in range(2):
        issue_scores(0, chunk_of(0, 0, k), k == 0, (0, k))
    zero_row = jnp.zeros((1, width), F32)
    carry = (zero_row, (zero_row, zero_row))
    carry = step(0, 0, 0, carry, first=True)
    lax.fori_loop(0, n_tiles, tile, carry)


def _mla_attn_kernel(q_ref, k_ref, v_ref, o_ref, s_buf, max_buf, p_buf, acc_buf):
    chunk, tq = s_buf.shape[2:]
    seq = k_ref.shape[2]

    def tile_cols(qi):
        return pl.ds(pl.multiple_of(qi * tq, tq), tq)

    def issue_scores(qi, c, is_first_chunk, slot):
        _store_scores(s_buf, max_buf, slot,
                      _dot(k_ref[0, 0, _key_rows(c, chunk), :], q_ref[0, 0, :, tile_cols(qi)]))

    def finalize(qi):
        o_ref[0, 0, :, tile_cols(qi)] = (
            acc_buf[0:MLA_V, :] / acc_buf[MLA_V:MLA_V + 1, :]).astype(BF16)

    _sweep_head(n_tiles=seq // tq, width=tq, window=lambda qi: (0, seq // chunk),
                first_chunk=lambda qi: 0, issue_scores=issue_scores,
                v_chunk=lambda c: v_ref[0, 0, :, _key_rows(c, chunk)], finalize=finalize,
                s_buf=s_buf, max_buf=max_buf, p_buf=p_buf, acc_buf=acc_buf)


def _diff_attn_kernel(reach_ref, q_ref, k_ref, v_ref, lq1_ref, lk1_ref, lq2_ref, lk2_ref, gsub_ref,
                      o_ref, s_buf, max_buf, p_buf, acc_buf, *, lambda_init):
    chunk, tq = s_buf.shape[2], s_buf.shape[3] // 2
    seq = k_ref.shape[2]
    n_chunks = seq // chunk
    assert chunk % tq == 0 and n_chunks % 4 == 0
    first_chunk = lambda qi: (qi * tq) // chunk
    reach = reach_ref[pl.program_id(0) * pl.num_programs(1) + pl.program_id(1)]

    def window(qi):
        lo = jnp.maximum(qi * tq - reach, 0) // chunk
        hi = jnp.minimum(qi * tq + tq - 1 + reach, seq - 1) // chunk
        count = jnp.minimum((hi - lo + 4) // 4 * 4, n_chunks)
        return jnp.minimum(lo, n_chunks - count), count

    lam = (jnp.exp(jnp.sum(lq1_ref[...] * lk1_ref[...], axis=-1, keepdims=True))
           - jnp.exp(jnp.sum(lq2_ref[...] * lk2_ref[...], axis=-1, keepdims=True)) + lambda_init)
    zero_rows = jnp.zeros((DIFF_K_WIDTH - 2 * DIFF_QK - AUG_ROWS, 2 * tq), BF16)

    def issue_scores(qi, c, is_first_chunk, slot):
        qa = q_ref[0, 0, :, pl.ds(pl.multiple_of(qi * 2 * tq, 2 * tq), 2 * tq)]
        kc = k_ref[0, 0, _key_rows(c, chunk), :]

        def scores(sign):
            aug = (qa[2 * DIFF_QK:].astype(F32) * sign).astype(BF16)
            return _dot(kc, jnp.concatenate([qa[:2 * DIFF_QK], aug, zero_rows], axis=0))

        if is_first_chunk:
            st = jnp.minimum(scores(1.0), scores(-1.0))
        else:
            st = scores(jnp.where(c < first_chunk(qi), 1.0, -1.0).astype(F32))
        _store_scores(s_buf, max_buf, slot, st)

    def finalize(qi):
        o_all = acc_buf[0:DIFF_V, :] / acc_buf[DIFF_V:DIFF_V + 1, :]
        o = o_all[:, :tq] - lam * o_all[:, tq:]
        o_ref[0, 0, :, pl.ds(pl.multiple_of(qi * tq, tq), tq)] = (
            _rms_cols(o, gsub_ref[...]) * (1.0 - lambda_init)).astype(BF16)

    _sweep_head(n_tiles=seq // tq, width=2 * tq, window=window,
                first_chunk=first_chunk, issue_scores=issue_scores,
                v_chunk=lambda c: v_ref[0, 0, :, _key_rows(c, chunk)], finalize=finalize,
                s_buf=s_buf, max_buf=max_buf, p_buf=p_buf, acc_buf=acc_buf)


def _post_kernel(h1_ref, om_ref, od_ref, p_ref, wom_ref, wod_ref, g2_ref, wg_ref, wu_ref, wd_ref,
                 gpi_ref, wpg_ref, bpg_ref, wpp_ref, gpo_ref, out_ref):
    h = h1_ref[...] + _dot_tn(om_ref[0], wom_ref[...]) + _dot_tn(od_ref[0], wod_ref[...])
    h = h + 0.5 * _swiglu(_rms_rows(h, g2_ref[...]).astype(BF16), wg_ref, wu_ref, wd_ref)
    z = _dot(_rms_rows(h, gpi_ref[...]).astype(BF16), wpg_ref[...]) + bpg_ref[...]
    gate = 1.0 / (1.0 + jnp.exp(-z))
    emb = _rms_rows(_dot(p_ref[...].astype(BF16), wpp_ref[...]), gpo_ref[...])
    out_ref[...] = h + gate * emb


def _resident(shape):
    nd = len(shape)
    return pl.BlockSpec(shape, lambda *_: (0,) * nd, pipeline_mode=pl.Buffered(1))


def _params(n_axes):
    return pltpu.CompilerParams(dimension_semantics=("arbitrary",) * n_axes,
                                vmem_limit_bytes=VMEM_LIMIT_BYTES)


def _attn_scratch(chunk, width, rows):
    return [pltpu.VMEM((2, 2, chunk, width), BF16), pltpu.VMEM((2, 2, 1, width), F32),
            pltpu.VMEM((2, 2, chunk, width), BF16), pltpu.VMEM((rows, width), F32)]


def _rope_tables_t(seq):
    pos = jnp.arange(seq, dtype=F32)
    inv = ROPE_THETA ** (-jnp.arange(0, MLA_ROPE, 2, dtype=F32) / MLA_ROPE)
    ang = pos[:, None] * inv[None, :]
    ang = jnp.concatenate([ang, ang], axis=-1)
    return jnp.cos(ang).T, jnp.sin(ang).T


def _alibi_reach(sq_norms, slopes2, batch, seq):
    n_maps = 2 * DIFF_HEADS
    top = jnp.max(sq_norms[:, :, 0].reshape(batch, -1, 2 * n_maps), axis=1)
    bound = jnp.sqrt(jnp.max((top[:, :n_maps] * top[:, n_maps:]).reshape(batch, DIFF_HEADS, 2), axis=-1))
    bound = bound * 1.02
    reach = jnp.ceil((2.0 * bound + 160.0) / slopes2[None, :])
    return jnp.minimum(reach, float(seq)).astype(jnp.int32).reshape(-1)


def _alibi_operands(slopes2, seq, tq):
    heads = slopes2.shape[0]
    hi = slopes2.astype(BF16).astype(F32)
    lo = slopes2 - hi
    pos = jnp.arange(seq, dtype=jnp.int32)
    blk = (pos // POS_BLOCK).astype(F32)
    rem = (pos % POS_BLOCK).astype(F32)
    per_head = lambda v: jnp.broadcast_to(v[:, None], (heads, seq))
    per_pos = lambda v: jnp.broadcast_to(v[None, :], (heads, seq))
    q_rows = [per_pos(blk), per_pos(blk), per_pos(rem), per_pos(rem),
              per_head(POS_BLOCK * hi), per_head(POS_BLOCK * lo), per_head(hi), per_head(lo)]
    k_cols = [per_head(-POS_BLOCK * hi), per_head(-POS_BLOCK * lo), per_head(-hi), per_head(-lo),
              per_pos(blk), per_pos(blk), per_pos(rem), per_pos(rem)]
    q_aug = jnp.stack(q_rows, axis=1)
    q_aug = jnp.pad(q_aug, ((0, 0), (0, AUG_ROWS - len(q_rows)), (0, 0)))
    q_aug = q_aug.reshape(heads, AUG_ROWS, seq // tq, 1, tq)
    q_aug = jnp.broadcast_to(q_aug, (heads, AUG_ROWS, seq // tq, 2, tq)).reshape(heads, AUG_ROWS, 2 * seq)
    k_aug = jnp.stack(k_cols, axis=2)
    k_aug = jnp.pad(k_aug, ((0, 0), (0, 0), (0, DIFF_K_WIDTH - 2 * DIFF_QK - len(k_cols))))
    return q_aug.astype(BF16), k_aug.astype(BF16)


def kernel(x, p, g_ffn1, w_ffn1_gate, w_ffn1_up, w_ffn1_down, g_mix, w_in, g_q_lat, w_q_up, g_kv_lat, w_kv_up, g_mla_q, g_mla_k, g_diff_q, g_diff_k, lambda_q1, lambda_k1, lambda_q2, lambda_k2, g_diff_sub, w_out, g_ffn2, w_ffn2_gate, w_ffn2_up, w_ffn2_down, g_ple_in, w_ple_gate, b_ple_gate, w_ple_proj, g_ple_out):
    batch, seq, d_model = x.shape
    depth = p.shape[0]
    tokens = batch * seq
    tm = TOKEN_TILE
    tiles_per_seq = seq // tm
    assert seq % tm == 0 and seq % MLA_Q_TILE == 0
    assert seq % (4 * MLA_KV_CHUNK) == 0 and seq % (4 * DIFF_KV_CHUNK) == 0
    assert tm == DIFF_Q_TILE == POS_BLOCK

    cos_t, sin_t = _rope_tables_t(seq)
    slopes2 = LOG2E * 2.0 ** (-8.0 * jnp.arange(1, DIFF_HEADS + 1, dtype=F32) / DIFF_HEADS)
    q_aug, k_aug = _alibi_operands(slopes2, seq, DIFF_Q_TILE)
    row = lambda v: v.reshape(1, -1)
    col = lambda v: v.reshape(-1, 1)
    bf = lambda w: w.astype(BF16)

    h = x.reshape(tokens, d_model)
    for i in range(depth):
        lambda_init = 0.8 - 0.6 * math.exp(-0.3 * i)
        ple_dim = p.shape[-1]

        tok_spec = pl.BlockSpec((tm, d_model), lambda t: (t, 0))
        head_t = lambda heads, rows, width: pl.BlockSpec(
            (1, heads, rows, width), lambda t: (t // tiles_per_seq, 0, 0, t % tiles_per_seq))
        head_r = lambda heads, width: pl.BlockSpec(
            (1, heads, tm, width), lambda t: (t // tiles_per_seq, 0, t % tiles_per_seq, 0))
        rope_spec = pl.BlockSpec((MLA_ROPE, tm), lambda t: (0, t % tiles_per_seq))
        pre_inputs = [
            (h, tok_spec),
            (row(g_ffn1[i]), None), (bf(w_ffn1_gate[i]), None), (bf(w_ffn1_up[i]), None),
            (bf(w_ffn1_down[i]), None), (row(g_mix[i]), None), (bf(w_in[i].T), None),
            (col(g_q_lat[i]), None), (bf(w_q_up[i].T), None),
            (col(g_kv_lat[i]), None), (bf(w_kv_up[i].T), None),
            (col(g_mla_q[i, :MLA_NOPE]), None), (col(g_mla_q[i, MLA_NOPE:]), None),
            (col(g_mla_k[i, :MLA_NOPE]), None), (col(g_mla_k[i, MLA_NOPE:]), None),
            (col(g_diff_q[i]), None), (col(g_diff_k[i]), None),
            (cos_t, rope_spec), (sin_t, rope_spec),
            (q_aug, pl.BlockSpec((DIFF_HEADS, AUG_ROWS, 2 * tm), lambda t: (0, 0, t % tiles_per_seq))),
            (k_aug, pl.BlockSpec((DIFF_HEADS, tm, DIFF_K_WIDTH - 2 * DIFF_QK),
                                 lambda t: (0, t % tiles_per_seq, 0))),
        ]
        dq_rows = 2 * DIFF_QK + AUG_ROWS
        h1, qm, km, vm, qd, kd, vd, sq_norms = pl.pallas_call(
            _pre_kernel,
            grid=(tokens // tm,),
            in_specs=[s if s is not None else _resident(a.shape) for a, s in pre_inputs],
            out_specs=[tok_spec, head_t(MLA_HEADS, QK_PAD, tm), head_r(MLA_HEADS, QK_PAD),
                       head_t(MLA_HEADS, MLA_V + ONES_ROWS, tm), head_t(DIFF_HEADS, dq_rows, 2 * tm),
                       head_r(DIFF_HEADS, DIFF_K_WIDTH), head_t(DIFF_HEADS, DIFF_V + ONES_ROWS, tm),
                       pl.BlockSpec((1, 4 * DIFF_HEADS, LANES), lambda t: (t, 0, 0))],
            out_shape=[
                jax.ShapeDtypeStruct((tokens, d_model), F32),
                jax.ShapeDtypeStruct((batch, MLA_HEADS, QK_PAD, seq), BF16),
                jax.ShapeDtypeStruct((batch, MLA_HEADS, seq, QK_PAD), BF16),
                jax.ShapeDtypeStruct((batch, MLA_HEADS, MLA_V + ONES_ROWS, seq), BF16),
                jax.ShapeDtypeStruct((batch, DIFF_HEADS, dq_rows, 2 * seq), BF16),
                jax.ShapeDtypeStruct((batch, DIFF_HEADS, seq, DIFF_K_WIDTH), BF16),
                jax.ShapeDtypeStruct((batch, DIFF_HEADS, DIFF_V + ONES_ROWS, seq), BF16),
                jax.ShapeDtypeStruct((tokens // tm, 4 * DIFF_HEADS, LANES), F32),
            ],
            compiler_params=_params(1),
            name="pre",
        )(*[a for a, _ in pre_inputs])

        per_head = lambda *tail: pl.BlockSpec((1, 1) + tail, lambda b, hh: (b, hh, 0, 0))
        om = pl.pallas_call(
            _mla_attn_kernel,
            grid=(batch, MLA_HEADS),
            in_specs=[per_head(QK_PAD, seq), per_head(seq, QK_PAD), per_head(MLA_V + ONES_ROWS, seq)],
            out_specs=per_head(MLA_V, seq),
            out_shape=jax.ShapeDtypeStruct((batch, MLA_HEADS, MLA_V, seq), BF16),
            scratch_shapes=_attn_scratch(MLA_KV_CHUNK, MLA_Q_TILE, MLA_V + ONES_ROWS),
            compiler_params=_params(2),
            name="mla_attn",
        )(qm, km, vm)

        small = lambda shape: pl.BlockSpec(shape, lambda b, hh: (0,) * len(shape))
        od = pl.pallas_call(
            functools.partial(_diff_attn_kernel, lambda_init=lambda_init),
            grid=(batch, DIFF_HEADS),
            in_specs=[pl.BlockSpec(memory_space=pltpu.SMEM),
                      per_head(dq_rows, 2 * seq), per_head(seq, DIFF_K_WIDTH),
                      per_head(DIFF_V + ONES_ROWS, seq)]
                     + [small((1, DIFF_QK))] * 4 + [small((DIFF_V, 1))],
            out_specs=per_head(DIFF_V, seq),
            out_shape=jax.ShapeDtypeStruct((batch, DIFF_HEADS, DIFF_V, seq), BF16),
            scratch_shapes=_attn_scratch(DIFF_KV_CHUNK, 2 * DIFF_Q_TILE, DIFF_V + ONES_ROWS),
            compiler_params=_params(2),
            name="diff_attn",
        )(_alibi_reach(sq_norms, slopes2, batch, seq), qd, kd, vd, row(lambda_q1[i]), row(lambda_k1[i]), row(lambda_q2[i]),
          row(lambda_k2[i]), col(g_diff_sub[i]))

        mla_w = MLA_HEADS * MLA_V
        diff_w = DIFF_HEADS * DIFF_V
        feat_t = lambda width: pl.BlockSpec(
            (1, width, tm), lambda t: (t // tiles_per_seq, 0, t % tiles_per_seq))
        post_inputs = [
            (h1, tok_spec),
            (om.reshape(batch, mla_w, seq), feat_t(mla_w)),
            (od.reshape(batch, diff_w, seq), feat_t(diff_w)),
            (p[i].reshape(tokens, ple_dim), pl.BlockSpec((tm, ple_dim), lambda t: (t, 0))),
            (bf(w_out[i, :mla_w]), None), (bf(w_out[i, mla_w:]), None),
            (row(g_ffn2[i]), None), (bf(w_ffn2_gate[i]), None), (bf(w_ffn2_up[i]), None),
            (bf(w_ffn2_down[i]), None),
            (row(g_ple_in[i]), None), (bf(w_ple_gate[i]), None), (row(b_ple_gate[i]), None),
            (bf(w_ple_proj[i]), None), (row(g_ple_out[i]), None),
        ]
        h = pl.pallas_call(
            _post_kernel,
            grid=(tokens // tm,),
            in_specs=[s if s is not None else _resident(a.shape) for a, s in post_inputs],
            out_specs=tok_spec,
            out_shape=jax.ShapeDtypeStruct((tokens, d_model), F32),
            compiler_params=_params(1),
            name="post",
        )(*[a for a, _ in post_inputs])

    return h.reshape(batch, seq, d_model)
```

```python
import functools
import math

import jax
import jax.numpy as jnp
import numpy as np
from jax import lax
from jax.experimental import pallas as pl
from jax.experimental.pallas import tpu as pltpu

F32 = jnp.float32
BF16 = jnp.bfloat16

MLA_HEADS = 8
MLA_NOPE = 64
MLA_ROPE = 32
MLA_QK = MLA_NOPE + MLA_ROPE
MLA_V = 64
MLA_Q_RANK = 256
MLA_KV_RANK = 128
ROPE_THETA = 10000.0
DIFF_HEADS = 4
DIFF_QK = 64
DIFF_V = 2 * DIFF_QK
EPS = 1e-6

QK_PAD = 128
ONES_ROWS = 16
AUG_ROWS = 16
DIFF_K_WIDTH = 256
POS_BLOCK = 256
NEG_BIG = -1e30
LANES = 128
LOG2E = math.log2(math.e)

VMEM_LIMIT_BYTES = 56 * 1024 * 1024
TOKEN_TILE = 256
POST_TOKEN_TILE = 512
MLA_Q_TILE = 1024
DIFF_Q_TILE = 256
MLA_KV_CHUNK = 512
DIFF_KV_CHUNK = 512


def _dot(a, b):
    return jnp.dot(a, b, preferred_element_type=F32)


def _dot_nt(a, b):
    return lax.dot_general(a, b, (((1,), (1,)), ((), ())), preferred_element_type=F32)


def _dot_tn(a, b):
    return lax.dot_general(a, b, (((0,), (0,)), ((), ())), preferred_element_type=F32)


def _rms_rows(x, g_row):
    ms = jnp.mean(x * x, axis=-1, keepdims=True)
    return x * lax.rsqrt(ms + EPS) * g_row


def _rms_cols(xt, g_col):
    ms = jnp.mean(xt * xt, axis=0, keepdims=True)
    return xt * lax.rsqrt(ms + EPS) * g_col


def _swiglu(xn, wg_ref, wu_ref, wd_ref):
    gate = _dot(xn, wg_ref[...])
    up = _dot(xn, wu_ref[...])
    act = (gate / (1.0 + jnp.exp(-gate))) * up
    return _dot(act.astype(BF16), wd_ref[...])


def _rope_cols(xt, cos, sin):
    half = MLA_ROPE // 2
    rot = jnp.concatenate([-xt[half:], xt[:half]], axis=0)
    return xt * cos + rot * sin


def _pre_kernel(x_ref, g1_ref, wg_ref, wu_ref, wd_ref, gmix_ref, win_t_ref,
                gql_ref, wqu_t_ref, gkvl_ref, wkvu_t_ref,
                gqn_ref, gqr_ref, gkn_ref, gkr_ref, gdq_ref, gdk_ref, cos_ref, sin_ref,
                h1_ref, qm_ref, km_ref, vm_ref, qd_ref, kd_ref, vd_ref, norm_ref,
                *, slope_parts, tiles_per_seq):
    t = x_ref.shape[0]
    x = x_ref[...]
    h1 = x + 0.5 * _swiglu(_rms_rows(x, g1_ref[...]).astype(BF16), wg_ref, wu_ref, wd_ref)
    h1_ref[...] = h1

    hn = _rms_rows(h1, gmix_ref[...]).astype(BF16)
    ut = _dot_nt(win_t_ref[...], hn)
    o_kv = MLA_Q_RANK
    o_kr = o_kv + MLA_KV_RANK
    o_qd = o_kr + MLA_ROPE
    o_kd = o_qd + DIFF_HEADS * 2 * DIFF_QK
    o_vd = o_kd + DIFF_HEADS * 2 * DIFF_QK

    cos = cos_ref[...]
    sin = sin_ref[...]
    ones = jnp.ones((ONES_ROWS, t), BF16)
    zeros_pad = jnp.zeros((QK_PAD - MLA_QK, t), F32)

    q_scale = MLA_QK ** -0.5 * LOG2E
    qln = _rms_cols(ut[0:o_kv], gql_ref[...]).astype(BF16)
    qt = _dot(wqu_t_ref[...], qln)
    kvn = _rms_cols(ut[o_kv:o_kr], gkvl_ref[...]).astype(BF16)
    kvt = _dot(wkvu_t_ref[...], kvn)
    k_rot = _rope_cols(_rms_cols(ut[o_kr:o_qd], gkr_ref[...]), cos, sin)
    for h in range(MLA_HEADS):
        qh = qt[h * MLA_QK:(h + 1) * MLA_QK]
        q_nope = _rms_cols(qh[:MLA_NOPE], gqn_ref[...])
        q_rot = _rope_cols(_rms_cols(qh[MLA_NOPE:], gqr_ref[...]), cos, sin)
        qm_ref[0, h, 0:MLA_NOPE, :] = (q_nope * q_scale).astype(BF16)
        qm_ref[0, h, MLA_NOPE:MLA_QK, :] = (q_rot * q_scale).astype(BF16)
        qm_ref[0, h, MLA_QK:QK_PAD, :] = zeros_pad.astype(BF16)
        kvh = kvt[h * (MLA_NOPE + MLA_V):(h + 1) * (MLA_NOPE + MLA_V)]
        k_nope = _rms_cols(kvh[:MLA_NOPE], gkn_ref[...])
        k_full = jnp.concatenate([k_nope, k_rot, zeros_pad], axis=0)
        km_ref[0, h] = k_full.T.astype(BF16)
        vm_ref[0, h, 0:MLA_V, :] = kvh[MLA_NOPE:].astype(BF16)
        vm_ref[0, h, MLA_V:MLA_V + ONES_ROWS, :] = ones

    d_scale = DIFF_QK ** -0.5 * LOG2E
    zero_q = jnp.zeros((DIFF_QK, t), BF16)
    n_maps = 2 * DIFF_HEADS

    assert t == POS_BLOCK
    blk = (pl.program_id(0) % tiles_per_seq).astype(F32)

    def pick(index, values):
        out = jnp.zeros(index.shape, F32)
        for n, v in enumerate(values):
            out = jnp.where(index == n, v, out)
        return out

    def q_aug(hi, lo):
        r = lax.broadcasted_iota(jnp.int32, (AUG_ROWS, 2 * t), 0)
        rem = (lax.broadcasted_iota(jnp.int32, (AUG_ROWS, 2 * t), 1) % t).astype(F32)
        return pick(r, [blk, blk, rem, rem, POS_BLOCK * hi, POS_BLOCK * lo, hi, lo]).astype(BF16)

    def k_aug(hi, lo):
        width = DIFF_K_WIDTH - 2 * DIFF_QK
        c = lax.broadcasted_iota(jnp.int32, (t, width), 1)
        rem = lax.broadcasted_iota(jnp.int32, (t, width), 0).astype(F32)
        return pick(c, [-POS_BLOCK * hi, -POS_BLOCK * lo, -hi, -lo, blk, blk, rem, rem]).astype(BF16)

    def put_max_sq_norm(row, xt):
        sq = jnp.max(jnp.sum(xt * xt, axis=0, keepdims=True), axis=1, keepdims=True)
        norm_ref[0, row:row + 1, :] = jnp.broadcast_to(sq, (1, norm_ref.shape[2]))

    for h in range(DIFF_HEADS):
        k_maps = []
        for j in range(2):
            r = (2 * h + j) * DIFF_QK
            qj = _rms_cols(ut[o_qd + r:o_qd + r + DIFF_QK], gdq_ref[...]) * d_scale
            kj = _rms_cols(ut[o_kd + r:o_kd + r + DIFF_QK], gdk_ref[...])
            put_max_sq_norm(2 * h + j, qj)
            put_max_sq_norm(n_maps + 2 * h + j, kj)
            qd_ref[0, h, j * DIFF_QK:(j + 1) * DIFF_QK, j * t:(j + 1) * t] = qj.astype(BF16)
            qd_ref[0, h, (1 - j) * DIFF_QK:(2 - j) * DIFF_QK, j * t:(j + 1) * t] = zero_q
            k_maps.append(kj)
        qd_ref[0, h, 2 * DIFF_QK:2 * DIFF_QK + AUG_ROWS, :] = q_aug(*slope_parts[h][1:])
        k12 = jnp.concatenate(k_maps, axis=0)
        kd_ref[0, h, :, 0:2 * DIFF_QK] = k12.T.astype(BF16)
        kd_ref[0, h, :, 2 * DIFF_QK:DIFF_K_WIDTH] = k_aug(*slope_parts[h][1:])
        vd_ref[0, h, 0:DIFF_V, :] = ut[o_vd + h * DIFF_V:o_vd + (h + 1) * DIFF_V].astype(BF16)
        vd_ref[0, h, DIFF_V:DIFF_V + ONES_ROWS, :] = ones


def _key_rows(c, size):
    return pl.ds(pl.multiple_of(c * size, size), size)


def _store_scores(s_buf, max_buf, slot, scores):
    sb = scores.astype(BF16)
    s_buf[slot] = sb
    max_buf[slot] = jnp.max(sb, axis=0, keepdims=True).astype(F32)


def _sweep_head(*, n_tiles, width, window, first_chunk, issue_scores, v_chunk, finalize,
                s_buf, max_buf, p_buf, acc_buf):
    def chunk_of(qi, i, k):
        lo, count = window(qi)
        return lo + (first_chunk(qi) - lo + 2 * i + k) % count

    def step(qi, i, parity, carry, first=False, last=False):
        m, alphas = carry
        cur = [(s_buf[parity, k], max_buf[parity, k]) for k in range(2)]

        nxt_qi, nxt_i = (qi + 1, 0) if last else (qi, i + 1)
        nxt_qi = jnp.minimum(nxt_qi, n_tiles - 1)
        for k in range(2):
            issue_scores(nxt_qi, chunk_of(nxt_qi, nxt_i, k), last and k == 0, (1 - parity, k))

        if first:
            prv_qi = jnp.maximum(qi - 1, 0)
            prv_i = window(prv_qi)[1] // 2 - 1
        else:
            prv_qi, prv_i = qi, i - 1
        for k in range(2):
            acc_buf[...] = (acc_buf[...] * alphas[k]
                            + _dot(v_chunk(chunk_of(prv_qi, prv_i, k)), p_buf[1 - parity, k]))

        if first:
            m = jnp.full((1, width), NEG_BIG, F32)
        alphas = []
        for k, (sb, chunk_max) in enumerate(cur):
            m_new = jnp.maximum(m, chunk_max)
            p_buf[parity, k] = jnp.exp2(sb - m_new.astype(BF16))
            alphas.append(jnp.exp2(m - m_new))
            m = m_new
        return m, tuple(alphas)

    def tile(qi, carry):
        def two_steps(j, cr):
            cr = step(qi, 2 * j + 1, 1, cr)
            return step(qi, 2 * j + 2, 0, cr)
        n_pairs = window(qi)[1] // 2
        carry = lax.fori_loop(0, n_pairs // 2 - 1, two_steps, carry)
        carry = step(qi, n_pairs - 1, 1, carry, last=True)
        carry = step(qi + 1, 0, 0, carry, first=True)
        finalize(qi)
        return carry

    p_buf[...] = jnp.zeros(p_buf.shape, p_buf.dtype)
    acc_buf[...] = jnp.zeros(acc_buf.shape, acc_buf.dtype)
    for k in range(2):
        issue_scores(0, chunk_of(0, 0, k), k == 0, (0, k))
    zero_row = jnp.zeros((1, width), F32)
    carry = (zero_row, (zero_row, zero_row))
    carry = step(0, 0, 0, carry, first=True)
    lax.fori_loop(0, n_tiles, tile, carry)


def _mla_attn_kernel(q_ref, k_ref, v_ref, o_ref, s_buf, max_buf, p_buf, acc_buf):
    chunk, tq = s_buf.shape[2:]
    seq = k_ref.shape[2]

    def tile_cols(qi):
        return pl.ds(pl.multiple_of(qi * tq, tq), tq)

    def issue_scores(qi, c, is_first_chunk, slot):
        _store_scores(s_buf, max_buf, slot,
                      _dot(k_ref[0, 0, _key_rows(c, chunk), :], q_ref[0, 0, :, tile_cols(qi)]))

    def finalize(qi):
        o_ref[0, 0, :, tile_cols(qi)] = (
            acc_buf[0:MLA_V, :] / acc_buf[MLA_V:MLA_V + 1, :]).astype(BF16)

    _sweep_head(n_tiles=seq // tq, width=tq, window=lambda qi: (0, seq // chunk),
                first_chunk=lambda qi: 0, issue_scores=issue_scores,
                v_chunk=lambda c: v_ref[0, 0, :, _key_rows(c, chunk)], finalize=finalize,
                s_buf=s_buf, max_buf=max_buf, p_buf=p_buf, acc_buf=acc_buf)


def _diff_attn_kernel(reach_ref, q_ref, k_ref, v_ref, lq1_ref, lk1_ref, lq2_ref, lk2_ref, gsub_ref,
                      o_ref, s_buf, max_buf, p_buf, acc_buf, *, lambda_init):
    chunk, tq = s_buf.shape[2], s_buf.shape[3] // 2
    seq = k_ref.shape[2]
    n_chunks = seq // chunk
    assert chunk % tq == 0 and n_chunks % 4 == 0
    first_chunk = lambda qi: (qi * tq) // chunk
    reach = reach_ref[pl.program_id(0) * pl.num_programs(1) + pl.program_id(1)]

    def window(qi):
        lo = jnp.maximum(qi * tq - reach, 0) // chunk
        hi = jnp.minimum(qi * tq + tq - 1 + reach, seq - 1) // chunk
        count = jnp.minimum((hi - lo + 4) // 4 * 4, n_chunks)
        return jnp.minimum(lo, n_chunks - count), count

    lam = (jnp.exp(jnp.sum(lq1_ref[...] * lk1_ref[...], axis=-1, keepdims=True))
           - jnp.exp(jnp.sum(lq2_ref[...] * lk2_ref[...], axis=-1, keepdims=True)) + lambda_init)
    zero_rows = jnp.zeros((DIFF_K_WIDTH - 2 * DIFF_QK - AUG_ROWS, 2 * tq), BF16)

    def issue_scores(qi, c, is_first_chunk, slot):
        qa = q_ref[0, 0, :, pl.ds(pl.multiple_of(qi * 2 * tq, 2 * tq), 2 * tq)]
        kc = k_ref[0, 0, _key_rows(c, chunk), :]

        def scores(sign):
            aug = (qa[2 * DIFF_QK:].astype(F32) * sign).astype(BF16)
            return _dot(kc, jnp.concatenate([qa[:2 * DIFF_QK], aug, zero_rows], axis=0))

        if is_first_chunk:
            st = jnp.minimum(scores(1.0), scores(-1.0))
        else:
            st = scores(jnp.where(c < first_chunk(qi), 1.0, -1.0).astype(F32))
        _store_scores(s_buf, max_buf, slot, st)

    def finalize(qi):
        o_all = acc_buf[0:DIFF_V, :] / acc_buf[DIFF_V:DIFF_V + 1, :]
        o = o_all[:, :tq] - lam * o_all[:, tq:]
        o_ref[0, 0, :, pl.ds(pl.multiple_of(qi * tq, tq), tq)] = (
            _rms_cols(o, gsub_ref[...]) * (1.0 - lambda_init)).astype(BF16)

    _sweep_head(n_tiles=seq // tq, width=2 * tq, window=window,
                first_chunk=first_chunk, issue_scores=issue_scores,
                v_chunk=lambda c: v_ref[0, 0, :, _key_rows(c, chunk)], finalize=finalize,
                s_buf=s_buf, max_buf=max_buf, p_buf=p_buf, acc_buf=acc_buf)


def _post_kernel(h1_ref, om_ref, od_ref, p_ref, wom_ref, wod_ref, g2_ref, wg_ref, wu_ref, wd_ref,
                 gpi_ref, wpg_ref, bpg_ref, wpp_ref, gpo_ref, out_ref):
    h = h1_ref[...] + _dot_tn(om_ref[0], wom_ref[...]) + _dot_tn(od_ref[0], wod_ref[...])
    h = h + 0.5 * _swiglu(_rms_rows(h, g2_ref[...]).astype(BF16), wg_ref, wu_ref, wd_ref)
    z = _dot(_rms_rows(h, gpi_ref[...]).astype(BF16), wpg_ref[...]) + bpg_ref[...]
    gate = 1.0 / (1.0 + jnp.exp(-z))
    emb = _rms_rows(_dot(p_ref[...].astype(BF16), wpp_ref[...]), gpo_ref[...])
    out_ref[...] = h + gate * emb


def _resident(shape):
    nd = len(shape)
    return pl.BlockSpec(shape, lambda *_: (0,) * nd, pipeline_mode=pl.Buffered(1))


def _params(n_axes):
    return pltpu.CompilerParams(dimension_semantics=("arbitrary",) * n_axes,
                                vmem_limit_bytes=VMEM_LIMIT_BYTES)


def _attn_scratch(chunk, width, rows):
    return [pltpu.VMEM((2, 2, chunk, width), BF16), pltpu.VMEM((2, 2, 1, width), F32),
            pltpu.VMEM((2, 2, chunk, width), BF16), pltpu.VMEM((rows, width), F32)]


def _rope_tables_t(seq):
    pos = jnp.arange(seq, dtype=F32)
    inv = ROPE_THETA ** (-jnp.arange(0, MLA_ROPE, 2, dtype=F32) / MLA_ROPE)
    ang = pos[:, None] * inv[None, :]
    ang = jnp.concatenate([ang, ang], axis=-1)
    return jnp.cos(ang).T, jnp.sin(ang).T


def _alibi_reach(sq_norms, slopes2, batch, seq):
    n_maps = 2 * DIFF_HEADS
    top = jnp.max(sq_norms[:, :, 0].reshape(batch, -1, 2 * n_maps), axis=1)
    bound = jnp.sqrt(jnp.max((top[:, :n_maps] * top[:, n_maps:]).reshape(batch, DIFF_HEADS, 2), axis=-1))
    bound = bound * 1.02
    reach = jnp.ceil((2.0 * bound + 160.0) / slopes2[None, :])
    return jnp.minimum(reach, float(seq)).astype(jnp.int32).reshape(-1)


def _alibi_slope_parts():
    parts = []
    for h in range(DIFF_HEADS):
        s2 = np.float32(LOG2E * 2.0 ** (-8.0 * (h + 1) / DIFF_HEADS))
        bits = s2.view(np.uint32)
        hi = np.uint32((bits + np.uint32(0x7FFF) + ((bits >> np.uint32(16)) & np.uint32(1)))
                       & np.uint32(0xFFFF0000)).view(np.float32)
        parts.append((float(s2), float(hi), float(s2 - hi)))
    return tuple(parts)


def kernel(x, p, g_ffn1, w_ffn1_gate, w_ffn1_up, w_ffn1_down, g_mix, w_in, g_q_lat, w_q_up, g_kv_lat, w_kv_up, g_mla_q, g_mla_k, g_diff_q, g_diff_k, lambda_q1, lambda_k1, lambda_q2, lambda_k2, g_diff_sub, w_out, g_ffn2, w_ffn2_gate, w_ffn2_up, w_ffn2_down, g_ple_in, w_ple_gate, b_ple_gate, w_ple_proj, g_ple_out):
    batch, seq, d_model = x.shape
    depth = p.shape[0]
    tokens = batch * seq
    tm = TOKEN_TILE
    tiles_per_seq = seq // tm
    assert seq % tm == 0 and seq % POST_TOKEN_TILE == 0 and seq % MLA_Q_TILE == 0
    assert seq % (4 * MLA_KV_CHUNK) == 0 and seq % (4 * DIFF_KV_CHUNK) == 0
    assert tm == DIFF_Q_TILE == POS_BLOCK

    cos_t, sin_t = _rope_tables_t(seq)
    slope_parts = _alibi_slope_parts()
    slopes2 = jnp.asarray([s2 for s2, _, _ in slope_parts], F32)
    row = lambda v: v.reshape(1, -1)
    col = lambda v: v.reshape(-1, 1)
    bf = lambda w: w.astype(BF16)

    h = x.reshape(tokens, d_model)
    for i in range(depth):
        lambda_init = 0.8 - 0.6 * math.exp(-0.3 * i)
        ple_dim = p.shape[-1]

        tok_spec = pl.BlockSpec((tm, d_model), lambda t: (t, 0))
        head_t = lambda heads, rows, width: pl.BlockSpec(
            (1, heads, rows, width), lambda t: (t // tiles_per_seq, 0, 0, t % tiles_per_seq))
        head_r = lambda heads, width: pl.BlockSpec(
            (1, heads, tm, width), lambda t: (t // tiles_per_seq, 0, t % tiles_per_seq, 0))
        rope_spec = pl.BlockSpec((MLA_ROPE, tm), lambda t: (0, t % tiles_per_seq))
        pre_inputs = [
            (h, tok_spec),
            (row(g_ffn1[i]), None), (bf(w_ffn1_gate[i]), None), (bf(w_ffn1_up[i]), None),
            (bf(w_ffn1_down[i]), None), (row(g_mix[i]), None), (bf(w_in[i].T), None),
            (col(g_q_lat[i]), None), (bf(w_q_up[i].T), None),
            (col(g_kv_lat[i]), None), (bf(w_kv_up[i].T), None),
            (col(g_mla_q[i, :MLA_NOPE]), None), (col(g_mla_q[i, MLA_NOPE:]), None),
            (col(g_mla_k[i, :MLA_NOPE]), None), (col(g_mla_k[i, MLA_NOPE:]), None),
            (col(g_diff_q[i]), None), (col(g_diff_k[i]), None),
            (cos_t, rope_spec), (sin_t, rope_spec),
        ]
        dq_rows = 2 * DIFF_QK + AUG_ROWS
        h1, qm, km, vm, qd, kd, vd, sq_norms = pl.pallas_call(
            functools.partial(_pre_kernel, slope_parts=slope_parts, tiles_per_seq=tiles_per_seq),
            grid=(tokens // tm,),
            in_specs=[s if s is not None else _resident(a.shape) for a, s in pre_inputs],
            out_specs=[tok_spec, head_t(MLA_HEADS, QK_PAD, tm), head_r(MLA_HEADS, QK_PAD),
                       head_t(MLA_HEADS, MLA_V + ONES_ROWS, tm), head_t(DIFF_HEADS, dq_rows, 2 * tm),
                       head_r(DIFF_HEADS, DIFF_K_WIDTH), head_t(DIFF_HEADS, DIFF_V + ONES_ROWS, tm),
                       pl.BlockSpec((1, 4 * DIFF_HEADS, LANES), lambda t: (t, 0, 0))],
            out_shape=[
                jax.ShapeDtypeStruct((tokens, d_model), F32),
                jax.ShapeDtypeStruct((batch, MLA_HEADS, QK_PAD, seq), BF16),
                jax.ShapeDtypeStruct((batch, MLA_HEADS, seq, QK_PAD), BF16),
                jax.ShapeDtypeStruct((batch, MLA_HEADS, MLA_V + ONES_ROWS, seq), BF16),
                jax.ShapeDtypeStruct((batch, DIFF_HEADS, dq_rows, 2 * seq), BF16),
                jax.ShapeDtypeStruct((batch, DIFF_HEADS, seq, DIFF_K_WIDTH), BF16),
                jax.ShapeDtypeStruct((batch, DIFF_HEADS, DIFF_V + ONES_ROWS, seq), BF16),
                jax.ShapeDtypeStruct((tokens // tm, 4 * DIFF_HEADS, LANES), F32),
            ],
            compiler_params=_params(1),
            name="pre",
        )(*[a for a, _ in pre_inputs])

        per_head = lambda *tail: pl.BlockSpec((1, 1) + tail, lambda b, hh: (b, hh, 0, 0))
        om = pl.pallas_call(
            _mla_attn_kernel,
            grid=(batch, MLA_HEADS),
            in_specs=[per_head(QK_PAD, seq), per_head(seq, QK_PAD), per_head(MLA_V + ONES_ROWS, seq)],
            out_specs=per_head(MLA_V, seq),
            out_shape=jax.ShapeDtypeStruct((batch, MLA_HEADS, MLA_V, seq), BF16),
            scratch_shapes=_attn_scratch(MLA_KV_CHUNK, MLA_Q_TILE, MLA_V + ONES_ROWS),
            compiler_params=_params(2),
            name="mla_attn",
        )(qm, km, vm)

        small = lambda shape: pl.BlockSpec(shape, lambda b, hh: (0,) * len(shape))
        od = pl.pallas_call(
            functools.partial(_diff_attn_kernel, lambda_init=lambda_init),
            grid=(batch, DIFF_HEADS),
            in_specs=[pl.BlockSpec(memory_space=pltpu.SMEM),
                      per_head(dq_rows, 2 * seq), per_head(seq, DIFF_K_WIDTH),
                      per_head(DIFF_V + ONES_ROWS, seq)]
                     + [small((1, DIFF_QK))] * 4 + [small((DIFF_V, 1))],
            out_specs=per_head(DIFF_V, seq),
            out_shape=jax.ShapeDtypeStruct((batch, DIFF_HEADS, DIFF_V, seq), BF16),
            scratch_shapes=_attn_scratch(DIFF_KV_CHUNK, 2 * DIFF_Q_TILE, DIFF_V + ONES_ROWS),
            compiler_params=_params(2),
            name="diff_attn",
        )(_alibi_reach(sq_norms, slopes2, batch, seq), qd, kd, vd, row(lambda_q1[i]), row(lambda_k1[i]), row(lambda_q2[i]),
          row(lambda_k2[i]), col(g_diff_sub[i]))

        mla_w = MLA_HEADS * MLA_V
        diff_w = DIFF_HEADS * DIFF_V
        tp = POST_TOKEN_TILE
        post_tiles_per_seq = seq // tp
        post_tok_spec = pl.BlockSpec((tp, d_model), lambda t: (t, 0))
        feat_t = lambda width: pl.BlockSpec(
            (1, width, tp), lambda t: (t // post_tiles_per_seq, 0, t % post_tiles_per_seq))
        post_inputs = [
            (h1, post_tok_spec),
            (om.reshape(batch, mla_w, seq), feat_t(mla_w)),
            (od.reshape(batch, diff_w, seq), feat_t(diff_w)),
            (p[i].reshape(tokens, ple_dim), pl.BlockSpec((tp, ple_dim), lambda t: (t, 0))),
            (bf(w_out[i, :mla_w]), None), (bf(w_out[i, mla_w:]), None),
            (row(g_ffn2[i]), None), (bf(w_ffn2_gate[i]), None), (bf(w_ffn2_up[i]), None),
            (bf(w_ffn2_down[i]), None),
            (row(g_ple_in[i]), None), (bf(w_ple_gate[i]), None), (row(b_ple_gate[i]), None),
            (bf(w_ple_proj[i]), None), (row(g_ple_out[i]), None),
        ]
        h = pl.pallas_call(
            _post_kernel,
            grid=(tokens // tp,),
            in_specs=[s if s is not None else _resident(a.shape) for a, s in post_inputs],
            out_specs=post_tok_spec,
            out_shape=jax.ShapeDtypeStruct((tokens, d_model), F32),
            compiler_params=_params(1),
            name="post",
        )(*[a for a, _ in post_inputs])

    return h.reshape(batch, seq, d_model)
```

```python
import functools
import math

import jax
import jax.numpy as jnp
import numpy as np
from jax import lax
from jax.experimental import pallas as pl
from jax.experimental.pallas import tpu as pltpu

F32 = jnp.float32
BF16 = jnp.bfloat16

MLA_HEADS = 8
MLA_NOPE = 64
MLA_ROPE = 32
MLA_QK = MLA_NOPE + MLA_ROPE
MLA_V = 64
MLA_Q_RANK = 256
MLA_KV_RANK = 128
ROPE_THETA = 10000.0
DIFF_HEADS = 4
DIFF_QK = 64
DIFF_V = 2 * DIFF_QK
EPS = 1e-6

QK_PAD = 128
ONES_ROWS = 16
AUG_ROWS = 16
DIFF_K_WIDTH = 256
POS_BLOCK = 256
NEG_BIG = -1e30
LANES = 128
LOG2E = math.log2(math.e)

VMEM_LIMIT_BYTES = 56 * 1024 * 1024
TOKEN_TILE = 256
POST_TOKEN_TILE = 512
MLA_Q_TILE = 2048
DIFF_Q_TILE = 256
MLA_KV_CHUNK = 512
DIFF_KV_CHUNK = 512


def _dot(a, b):
    return jnp.dot(a, b, preferred_element_type=F32)


def _dot_nt(a, b):
    return lax.dot_general(a, b, (((1,), (1,)), ((), ())), preferred_element_type=F32)


def _dot_tn(a, b):
    return lax.dot_general(a, b, (((0,), (0,)), ((), ())), preferred_element_type=F32)


def _rms_rows(x, g_row):
    ms = jnp.mean(x * x, axis=-1, keepdims=True)
    return x * lax.rsqrt(ms + EPS) * g_row


def _rms_cols(xt, g_col):
    ms = jnp.mean(xt * xt, axis=0, keepdims=True)
    return xt * lax.rsqrt(ms + EPS) * g_col


def _swiglu(xn, wg_ref, wu_ref, wd_ref):
    gate = _dot(xn, wg_ref[...])
    up = _dot(xn, wu_ref[...])
    act = (gate / (1.0 + jnp.exp(-gate))) * up
    return _dot(act.astype(BF16), wd_ref[...])


def _rope_cols(xt, cos, sin):
    half = MLA_ROPE // 2
    rot = jnp.concatenate([-xt[half:], xt[:half]], axis=0)
    return xt * cos + rot * sin


def _pre_kernel(x_ref, g1_ref, wg_ref, wu_ref, wd_ref, gmix_ref, win_t_ref,
                gql_ref, wqu_t_ref, gkvl_ref, wkvu_t_ref,
                gqn_ref, gqr_ref, gkn_ref, gkr_ref, gdq_ref, gdk_ref, cos_ref, sin_ref,
                h1_ref, qm_ref, km_ref, vm_ref, qd_ref, kd_ref, vd_ref, norm_ref,
                *, slope_parts, tiles_per_seq):
    t = x_ref.shape[0]
    x = x_ref[...]
    h1 = x + 0.5 * _swiglu(_rms_rows(x, g1_ref[...]).astype(BF16), wg_ref, wu_ref, wd_ref)
    h1_ref[...] = h1

    hn = _rms_rows(h1, gmix_ref[...]).astype(BF16)
    ut = _dot_nt(win_t_ref[...], hn)
    o_kv = MLA_Q_RANK
    o_kr = o_kv + MLA_KV_RANK
    o_qd = o_kr + MLA_ROPE
    o_kd = o_qd + DIFF_HEADS * 2 * DIFF_QK
    o_vd = o_kd + DIFF_HEADS * 2 * DIFF_QK

    cos = cos_ref[...]
    sin = sin_ref[...]
    ones = jnp.ones((ONES_ROWS, t), BF16)
    zeros_pad = jnp.zeros((QK_PAD - MLA_QK, t), F32)

    q_scale = MLA_QK ** -0.5 * LOG2E
    qln = _rms_cols(ut[0:o_kv], gql_ref[...]).astype(BF16)
    qt = _dot(wqu_t_ref[...], qln)
    kvn = _rms_cols(ut[o_kv:o_kr], gkvl_ref[...]).astype(BF16)
    kvt = _dot(wkvu_t_ref[...], kvn)
    k_rot = _rope_cols(_rms_cols(ut[o_kr:o_qd], gkr_ref[...]), cos, sin)
    for h in range(MLA_HEADS):
        qh = qt[h * MLA_QK:(h + 1) * MLA_QK]
        q_nope = _rms_cols(qh[:MLA_NOPE], gqn_ref[...])
        q_rot = _rope_cols(_rms_cols(qh[MLA_NOPE:], gqr_ref[...]), cos, sin)
        qm_ref[0, h, 0:MLA_NOPE, :] = (q_nope * q_scale).astype(BF16)
        qm_ref[0, h, MLA_NOPE:MLA_QK, :] = (q_rot * q_scale).astype(BF16)
        qm_ref[0, h, MLA_QK:QK_PAD, :] = zeros_pad.astype(BF16)
        kvh = kvt[h * (MLA_NOPE + MLA_V):(h + 1) * (MLA_NOPE + MLA_V)]
        k_nope = _rms_cols(kvh[:MLA_NOPE], gkn_ref[...])
        k_full = jnp.concatenate([k_nope, k_rot, zeros_pad], axis=0)
        km_ref[0, h] = k_full.T.astype(BF16)
        vm_ref[0, h, 0:MLA_V, :] = kvh[MLA_NOPE:].astype(BF16)
        vm_ref[0, h, MLA_V:MLA_V + ONES_ROWS, :] = ones

    d_scale = DIFF_QK ** -0.5 * LOG2E
    zero_q = jnp.zeros((DIFF_QK, t), BF16)
    n_maps = 2 * DIFF_HEADS

    assert t == POS_BLOCK
    blk = (pl.program_id(0) % tiles_per_seq).astype(F32)

    def pick(index, values):
        out = jnp.zeros(index.shape, F32)
        for n, v in enumerate(values):
            out = jnp.where(index == n, v, out)
        return out

    def q_aug(hi, lo):
        r = lax.broadcasted_iota(jnp.int32, (AUG_ROWS, 2 * t), 0)
        rem = (lax.broadcasted_iota(jnp.int32, (AUG_ROWS, 2 * t), 1) % t).astype(F32)
        return pick(r, [blk, blk, rem, rem, POS_BLOCK * hi, POS_BLOCK * lo, hi, lo]).astype(BF16)

    def k_aug(hi, lo):
        width = DIFF_K_WIDTH - 2 * DIFF_QK
        c = lax.broadcasted_iota(jnp.int32, (t, width), 1)
        rem = lax.broadcasted_iota(jnp.int32, (t, width), 0).astype(F32)
        return pick(c, [-POS_BLOCK * hi, -POS_BLOCK * lo, -hi, -lo, blk, blk, rem, rem]).astype(BF16)

    def put_max_sq_norm(row, xt):
        sq = jnp.max(jnp.sum(xt * xt, axis=0, keepdims=True), axis=1, keepdims=True)
        norm_ref[0, row:row + 1, :] = jnp.broadcast_to(sq, (1, norm_ref.shape[2]))

    for h in range(DIFF_HEADS):
        k_maps = []
        for j in range(2):
            r = (2 * h + j) * DIFF_QK
            qj = _rms_cols(ut[o_qd + r:o_qd + r + DIFF_QK], gdq_ref[...]) * d_scale
            kj = _rms_cols(ut[o_kd + r:o_kd + r + DIFF_QK], gdk_ref[...])
            put_max_sq_norm(2 * h + j, qj)
            put_max_sq_norm(n_maps + 2 * h + j, kj)
            qd_ref[0, h, j * DIFF_QK:(j + 1) * DIFF_QK, j * t:(j + 1) * t] = qj.astype(BF16)
            qd_ref[0, h, (1 - j) * DIFF_QK:(2 - j) * DIFF_QK, j * t:(j + 1) * t] = zero_q
            k_maps.append(kj)
        qd_ref[0, h, 2 * DIFF_QK:2 * DIFF_QK + AUG_ROWS, :] = q_aug(*slope_parts[h][1:])
        k12 = jnp.concatenate(k_maps, axis=0)
        kd_ref[0, h, :, 0:2 * DIFF_QK] = k12.T.astype(BF16)
        kd_ref[0, h, :, 2 * DIFF_QK:DIFF_K_WIDTH] = k_aug(*slope_parts[h][1:])
        vd_ref[0, h, 0:DIFF_V, :] = ut[o_vd + h * DIFF_V:o_vd + (h + 1) * DIFF_V].astype(BF16)
        vd_ref[0, h, DIFF_V:DIFF_V + ONES_ROWS, :] = ones


def _key_rows(c, size):
    return pl.ds(pl.multiple_of(c * size, size), size)


def _store_scores(s_buf, max_buf, slot, scores):
    sb = scores.astype(BF16)
    s_buf[slot] = sb
    max_buf[slot] = jnp.max(sb, axis=0, keepdims=True).astype(F32)


def _sweep_head(*, n_tiles, width, window, first_chunk, issue_scores, v_chunk, finalize,
                s_buf, max_buf, p_buf, acc_buf):
    def chunk_of(qi, i, k):
        lo, count = window(qi)
        return lo + (first_chunk(qi) - lo + 2 * i + k) % count

    def step(qi, i, parity, carry, first=False, last=False):
        m, alphas = carry
        cur = [(s_buf[parity, k], max_buf[parity, k]) for k in range(2)]

        nxt_qi, nxt_i = (qi + 1, 0) if last else (qi, i + 1)
        nxt_qi = jnp.minimum(nxt_qi, n_tiles - 1)
        for k in range(2):
            issue_scores(nxt_qi, chunk_of(nxt_qi, nxt_i, k), last and k == 0, (1 - parity, k))

        if first:
            prv_qi = jnp.maximum(qi - 1, 0)
            prv_i = window(prv_qi)[1] // 2 - 1
        else:
            prv_qi, prv_i = qi, i - 1
        for k in range(2):
            acc_buf[...] = (acc_buf[...] * alphas[k]
                            + _dot(v_chunk(chunk_of(prv_qi, prv_i, k)), p_buf[1 - parity, k]))

        if first:
            m = jnp.full((1, width), NEG_BIG, F32)
        alphas = []
        for k, (sb, chunk_max) in enumerate(cur):
            m_new = jnp.maximum(m, chunk_max)
            p_buf[parity, k] = jnp.exp2(sb - m_new.astype(BF16))
            alphas.append(jnp.exp2(m - m_new))
            m = m_new
        return m, tuple(alphas)

    def tile(qi, carry):
        def two_steps(j, cr):
            cr = step(qi, 2 * j + 1, 1, cr)
            return step(qi, 2 * j + 2, 0, cr)
        n_pairs = window(qi)[1] // 2
        carry = lax.fori_loop(0, n_pairs // 2 - 1, two_steps, carry)
        carry = step(qi, n_pairs - 1, 1, carry, last=True)
        carry = step(qi + 1, 0, 0, carry, first=True)
        finalize(qi)
        return carry

    p_buf[...] = jnp.zeros(p_buf.shape, p_buf.dtype)
    acc_buf[...] = jnp.zeros(acc_buf.shape, acc_buf.dtype)
    for k in range(2):
        issue_scores(0, chunk_of(0, 0, k), k == 0, (0, k))
    zero_row = jnp.zeros((1, width), F32)
    carry = (zero_row, (zero_row, zero_row))
    carry = step(0, 0, 0, carry, first=True)
    lax.fori_loop(0, n_tiles, tile, carry)


def _mla_attn_kernel(q_ref, k_ref, v_ref, o_ref, s_buf, max_buf, p_buf, acc_buf):
    chunk, tq = s_buf.shape[2:]
    seq = k_ref.shape[2]

    def tile_cols(qi):
        return pl.ds(pl.multiple_of(qi * tq, tq), tq)

    def issue_scores(qi, c, is_first_chunk, slot):
        _store_scores(s_buf, max_buf, slot,
                      _dot(k_ref[0, 0, _key_rows(c, chunk), :], q_ref[0, 0, :, tile_cols(qi)]))

    def finalize(qi):
        o_ref[0, 0, :, tile_cols(qi)] = (
            acc_buf[0:MLA_V, :] / acc_buf[MLA_V:MLA_V + 1, :]).astype(BF16)

    _sweep_head(n_tiles=seq // tq, width=tq, window=lambda qi: (0, seq // chunk),
                first_chunk=lambda qi: 0, issue_scores=issue_scores,
                v_chunk=lambda c: v_ref[0, 0, :, _key_rows(c, chunk)], finalize=finalize,
                s_buf=s_buf, max_buf=max_buf, p_buf=p_buf, acc_buf=acc_buf)


def _diff_attn_kernel(reach_ref, q_ref, k_ref, v_ref, lq1_ref, lk1_ref, lq2_ref, lk2_ref, gsub_ref,
                      o_ref, s_buf, max_buf, p_buf, acc_buf, *, lambda_init):
    chunk, tq = s_buf.shape[2], s_buf.shape[3] // 2
    seq = k_ref.shape[2]
    n_chunks = seq // chunk
    assert chunk % tq == 0 and n_chunks % 4 == 0
    first_chunk = lambda qi: (qi * tq) // chunk
    reach = reach_ref[pl.program_id(0) * pl.num_programs(1) + pl.program_id(1)]

    def window(qi):
        lo = jnp.maximum(qi * tq - reach, 0) // chunk
        hi = jnp.minimum(qi * tq + tq - 1 + reach, seq - 1) // chunk
        count = jnp.minimum((hi - lo + 4) // 4 * 4, n_chunks)
        return jnp.minimum(lo, n_chunks - count), count

    lam = (jnp.exp(jnp.sum(lq1_ref[...] * lk1_ref[...], axis=-1, keepdims=True))
           - jnp.exp(jnp.sum(lq2_ref[...] * lk2_ref[...], axis=-1, keepdims=True)) + lambda_init)
    zero_rows = jnp.zeros((DIFF_K_WIDTH - 2 * DIFF_QK - AUG_ROWS, 2 * tq), BF16)

    def issue_scores(qi, c, is_first_chunk, slot):
        qa = q_ref[0, 0, :, pl.ds(pl.multiple_of(qi * 2 * tq, 2 * tq), 2 * tq)]
        kc = k_ref[0, 0, _key_rows(c, chunk), :]

        def scores(sign):
            aug = (qa[2 * DIFF_QK:].astype(F32) * sign).astype(BF16)
            return _dot(kc, jnp.concatenate([qa[:2 * DIFF_QK], aug, zero_rows], axis=0))

        if is_first_chunk:
            st = jnp.minimum(scores(1.0), scores(-1.0))
        else:
            st = scores(jnp.where(c < first_chunk(qi), 1.0, -1.0).astype(F32))
        _store_scores(s_buf, max_buf, slot, st)

    def finalize(qi):
        o_all = acc_buf[0:DIFF_V, :] / acc_buf[DIFF_V:DIFF_V + 1, :]
        o = o_all[:, :tq] - lam * o_all[:, tq:]
        o_ref[0, 0, :, pl.ds(pl.multiple_of(qi * tq, tq), tq)] = (
            _rms_cols(o, gsub_ref[...]) * (1.0 - lambda_init)).astype(BF16)

    _sweep_head(n_tiles=seq // tq, width=2 * tq, window=window,
                first_chunk=first_chunk, issue_scores=issue_scores,
                v_chunk=lambda c: v_ref[0, 0, :, _key_rows(c, chunk)], finalize=finalize,
                s_buf=s_buf, max_buf=max_buf, p_buf=p_buf, acc_buf=acc_buf)


def _post_kernel(h1_ref, om_ref, od_ref, p_ref, wom_ref, wod_ref, g2_ref, wg_ref, wu_ref, wd_ref,
                 gpi_ref, wpg_ref, bpg_ref, wpp_ref, gpo_ref, out_ref):
    h = h1_ref[...] + _dot_tn(om_ref[0], wom_ref[...]) + _dot_tn(od_ref[0], wod_ref[...])
    h = h + 0.5 * _swiglu(_rms_rows(h, g2_ref[...]).astype(BF16), wg_ref, wu_ref, wd_ref)
    z = _dot(_rms_rows(h, gpi_ref[...]).astype(BF16), wpg_ref[...]) + bpg_ref[...]
    gate = 1.0 / (1.0 + jnp.exp(-z))
    emb = _rms_rows(_dot(p_ref[...].astype(BF16), wpp_ref[...]), gpo_ref[...])
    out_ref[...] = h + gate * emb


def _resident(shape):
    nd = len(shape)
    return pl.BlockSpec(shape, lambda *_: (0,) * nd, pipeline_mode=pl.Buffered(1))


def _params(n_axes):
    return pltpu.CompilerParams(dimension_semantics=("arbitrary",) * n_axes,
                                vmem_limit_bytes=VMEM_LIMIT_BYTES)


def _attn_scratch(chunk, width, rows):
    return [pltpu.VMEM((2, 2, chunk, width), BF16), pltpu.VMEM((2, 2, 1, width), F32),
            pltpu.VMEM((2, 2, chunk, width), BF16), pltpu.VMEM((rows, width), F32)]


def _rope_tables_t(seq):
    pos = jnp.arange(seq, dtype=F32)
    inv = ROPE_THETA ** (-jnp.arange(0, MLA_ROPE, 2, dtype=F32) / MLA_ROPE)
    ang = pos[:, None] * inv[None, :]
    ang = jnp.concatenate([ang, ang], axis=-1)
    return jnp.cos(ang).T, jnp.sin(ang).T


def _alibi_reach(sq_norms, slopes2, batch, seq):
    n_maps = 2 * DIFF_HEADS
    top = jnp.max(sq_norms[:, :, 0].reshape(batch, -1, 2 * n_maps), axis=1)
    bound = jnp.sqrt(jnp.max((top[:, :n_maps] * top[:, n_maps:]).reshape(batch, DIFF_HEADS, 2), axis=-1))
    bound = bound * 1.02
    reach = jnp.ceil((2.0 * bound + 160.0) / slopes2[None, :])
    return jnp.minimum(reach, float(seq)).astype(jnp.int32).reshape(-1)


def _alibi_slope_parts():
    parts = []
    for h in range(DIFF_HEADS):
        s2 = np.float32(LOG2E * 2.0 ** (-8.0 * (h + 1) / DIFF_HEADS))
        bits = s2.view(np.uint32)
        hi = np.uint32((bits + np.uint32(0x7FFF) + ((bits >> np.uint32(16)) & np.uint32(1)))
                       & np.uint32(0xFFFF0000)).view(np.float32)
        parts.append((float(s2), float(hi), float(s2 - hi)))
    return tuple(parts)


def kernel(x, p, g_ffn1, w_ffn1_gate, w_ffn1_up, w_ffn1_down, g_mix, w_in, g_q_lat, w_q_up, g_kv_lat, w_kv_up, g_mla_q, g_mla_k, g_diff_q, g_diff_k, lambda_q1, lambda_k1, lambda_q2, lambda_k2, g_diff_sub, w_out, g_ffn2, w_ffn2_gate, w_ffn2_up, w_ffn2_down, g_ple_in, w_ple_gate, b_ple_gate, w_ple_proj, g_ple_out):
    batch, seq, d_model = x.shape
    depth = p.shape[0]
    tokens = batch * seq
    tm = TOKEN_TILE
    tiles_per_seq = seq // tm
    assert seq % tm == 0 and seq % POST_TOKEN_TILE == 0 and seq % MLA_Q_TILE == 0
    assert seq % (4 * MLA_KV_CHUNK) == 0 and seq % (4 * DIFF_KV_CHUNK) == 0
    assert tm == DIFF_Q_TILE == POS_BLOCK

    cos_t, sin_t = _rope_tables_t(seq)
    slope_parts = _alibi_slope_parts()
    slopes2 = jnp.asarray([s2 for s2, _, _ in slope_parts], F32)
    row = lambda v: v.reshape(1, -1)
    col = lambda v: v.reshape(-1, 1)
    bf = lambda w: w.astype(BF16)

    h = x.reshape(tokens, d_model)
    for i in range(depth):
        lambda_init = 0.8 - 0.6 * math.exp(-0.3 * i)
        ple_dim = p.shape[-1]

        tok_spec = pl.BlockSpec((tm, d_model), lambda t: (t, 0))
        head_t = lambda heads, rows, width: pl.BlockSpec(
            (1, heads, rows, width), lambda t: (t // tiles_per_seq, 0, 0, t % tiles_per_seq))
        head_r = lambda heads, width: pl.BlockSpec(
            (1, heads, tm, width), lambda t: (t // tiles_per_seq, 0, t % tiles_per_seq, 0))
        rope_spec = pl.BlockSpec((MLA_ROPE, tm), lambda t: (0, t % tiles_per_seq))
        pre_inputs = [
            (h, tok_spec),
            (row(g_ffn1[i]), None), (bf(w_ffn1_gate[i]), None), (bf(w_ffn1_up[i]), None),
            (bf(w_ffn1_down[i]), None), (row(g_mix[i]), None), (bf(w_in[i].T), None),
            (col(g_q_lat[i]), None), (bf(w_q_up[i].T), None),
            (col(g_kv_lat[i]), None), (bf(w_kv_up[i].T), None),
            (col(g_mla_q[i, :MLA_NOPE]), None), (col(g_mla_q[i, MLA_NOPE:]), None),
            (col(g_mla_k[i, :MLA_NOPE]), None), (col(g_mla_k[i, MLA_NOPE:]), None),
            (col(g_diff_q[i]), None), (col(g_diff_k[i]), None),
            (cos_t, rope_spec), (sin_t, rope_spec),
        ]
        dq_rows = 2 * DIFF_QK + AUG_ROWS
        h1, qm, km, vm, qd, kd, vd, sq_norms = pl.pallas_call(
            functools.partial(_pre_kernel, slope_parts=slope_parts, tiles_per_seq=tiles_per_seq),
            grid=(tokens // tm,),
            in_specs=[s if s is not None else _resident(a.shape) for a, s in pre_inputs],
            out_specs=[tok_spec, head_t(MLA_HEADS, QK_PAD, tm), head_r(MLA_HEADS, QK_PAD),
                       head_t(MLA_HEADS, MLA_V + ONES_ROWS, tm), head_t(DIFF_HEADS, dq_rows, 2 * tm),
                       head_r(DIFF_HEADS, DIFF_K_WIDTH), head_t(DIFF_HEADS, DIFF_V + ONES_ROWS, tm),
                       pl.BlockSpec((1, 4 * DIFF_HEADS, LANES), lambda t: (t, 0, 0))],
            out_shape=[
                jax.ShapeDtypeStruct((tokens, d_model), F32),
                jax.ShapeDtypeStruct((batch, MLA_HEADS, QK_PAD, seq), BF16),
                jax.ShapeDtypeStruct((batch, MLA_HEADS, seq, QK_PAD), BF16),
                jax.ShapeDtypeStruct((batch, MLA_HEADS, MLA_V + ONES_ROWS, seq), BF16),
                jax.ShapeDtypeStruct((batch, DIFF_HEADS, dq_rows, 2 * seq), BF16),
                jax.ShapeDtypeStruct((batch, DIFF_HEADS, seq, DIFF_K_WIDTH), BF16),
                jax.ShapeDtypeStruct((batch, DIFF_HEADS, DIFF_V + ONES_ROWS, seq), BF16),
                jax.ShapeDtypeStruct((tokens // tm, 4 * DIFF_HEADS, LANES), F32),
            ],
            compiler_params=_params(1),
            name="pre",
        )(*[a for a, _ in pre_inputs])

        per_head = lambda *tail: pl.BlockSpec((1, 1) + tail, lambda b, hh: (b, hh, 0, 0))
        om = pl.pallas_call(
            _mla_attn_kernel,
            grid=(batch, MLA_HEADS),
            in_specs=[per_head(QK_PAD, seq), per_head(seq, QK_PAD), per_head(MLA_V + ONES_ROWS, seq)],
            out_specs=per_head(MLA_V, seq),
            out_shape=jax.ShapeDtypeStruct((batch, MLA_HEADS, MLA_V, seq), BF16),
            scratch_shapes=_attn_scratch(MLA_KV_CHUNK, MLA_Q_TILE, MLA_V + ONES_ROWS),
            compiler_params=_params(2),
            name="mla_attn",
        )(qm, km, vm)

        small = lambda shape: pl.BlockSpec(shape, lambda b, hh: (0,) * len(shape))
        od = pl.pallas_call(
            functools.partial(_diff_attn_kernel, lambda_init=lambda_init),
            grid=(batch, DIFF_HEADS),
            in_specs=[pl.BlockSpec(memory_space=pltpu.SMEM),
                      per_head(dq_rows, 2 * seq), per_head(seq, DIFF_K_WIDTH),
                      per_head(DIFF_V + ONES_ROWS, seq)]
                     + [small((1, DIFF_QK))] * 4 + [small((DIFF_V, 1))],
            out_specs=per_head(DIFF_V, seq),
            out_shape=jax.ShapeDtypeStruct((batch, DIFF_HEADS, DIFF_V, seq), BF16),
            scratch_shapes=_attn_scratch(DIFF_KV_CHUNK, 2 * DIFF_Q_TILE, DIFF_V + ONES_ROWS),
            compiler_params=_params(2),
            name="diff_attn",
        )(_alibi_reach(sq_norms, slopes2, batch, seq), qd, kd, vd, row(lambda_q1[i]), row(lambda_k1[i]), row(lambda_q2[i]),
          row(lambda_k2[i]), col(g_diff_sub[i]))

        mla_w = MLA_HEADS * MLA_V
        diff_w = DIFF_HEADS * DIFF_V
        tp = POST_TOKEN_TILE
        post_tiles_per_seq = seq // tp
        post_tok_spec = pl.BlockSpec((tp, d_model), lambda t: (t, 0))
        feat_t = lambda width: pl.BlockSpec(
            (1, width, tp), lambda t: (t // post_tiles_per_seq, 0, t % post_tiles_per_seq))
        post_inputs = [
            (h1, post_tok_spec),
            (om.reshape(batch, mla_w, seq), feat_t(mla_w)),
            (od.reshape(batch, diff_w, seq), feat_t(diff_w)),
            (p[i].reshape(tokens, ple_dim), pl.BlockSpec((tp, ple_dim), lambda t: (t, 0))),
            (bf(w_out[i, :mla_w]), None), (bf(w_out[i, mla_w:]), None),
            (row(g_ffn2[i]), None), (bf(w_ffn2_gate[i]), None), (bf(w_ffn2_up[i]), None),
            (bf(w_ffn2_down[i]), None),
            (row(g_ple_in[i]), None), (bf(w_ple_gate[i]), None), (row(b_ple_gate[i]), None),
            (bf(w_ple_proj[i]), None), (row(g_ple_out[i]), None),
        ]
        h = pl.pallas_call(
            _post_kernel,
            grid=(tokens // tp,),
            in_specs=[s if s is not None else _resident(a.shape) for a, s in post_inputs],
            out_specs=post_tok_spec,
            out_shape=jax.ShapeDtypeStruct((tokens, d_model), F32),
            compiler_params=_params(1),
            name="post",
        )(*[a for a, _ in post_inputs])

    return h.reshape(batch, seq, d_model)
```

```python
import functools
import math

import jax
import jax.numpy as jnp
import numpy as np
from jax import lax
from jax.experimental import pallas as pl
from jax.experimental.pallas import tpu as pltpu

F32 = jnp.float32
BF16 = jnp.bfloat16

MLA_HEADS = 8
MLA_NOPE = 64
MLA_ROPE = 32
MLA_QK = MLA_NOPE + MLA_ROPE
MLA_V = 64
MLA_Q_RANK = 256
MLA_KV_RANK = 128
ROPE_THETA = 10000.0
DIFF_HEADS = 4
DIFF_QK = 64
DIFF_V = 2 * DIFF_QK
EPS = 1e-6

QK_PAD = 128
ONES_ROWS = 16
AUG_ROWS = 16
DIFF_K_WIDTH = 256
POS_BLOCK = 256
NEG_BIG = -1e30
WEIGHT_CUTOFF_BITS = 130.0
LANES = 128
LOG2E = math.log2(math.e)

VMEM_LIMIT_BYTES = 56 * 1024 * 1024
TOKEN_TILE = 256
POST_TOKEN_TILE = 512
MLA_Q_TILE = 1024
DIFF_Q_TILE = 512
MLA_KV_CHUNK = 512
DIFF_KV_CHUNK = 512


def _dot(a, b):
    return jnp.dot(a, b, preferred_element_type=F32)


def _dot_nt(a, b):
    return lax.dot_general(a, b, (((1,), (1,)), ((), ())), preferred_element_type=F32)


def _dot_tn(a, b):
    return lax.dot_general(a, b, (((0,), (0,)), ((), ())), preferred_element_type=F32)


def _rms_rows(x, g_row):
    ms = jnp.mean(x * x, axis=-1, keepdims=True)
    return x * lax.rsqrt(ms + EPS) * g_row


def _rms_cols(xt, g_col):
    ms = jnp.mean(xt * xt, axis=0, keepdims=True)
    return xt * lax.rsqrt(ms + EPS) * g_col


def _swiglu(xn, wg_ref, wu_ref, wd_ref):
    gate = _dot(xn, wg_ref[...])
    up = _dot(xn, wu_ref[...])
    act = (gate / (1.0 + jnp.exp(-gate))) * up
    return _dot(act.astype(BF16), wd_ref[...])


def _rope_cols(xt, cos, sin):
    half = MLA_ROPE // 2
    rot = jnp.concatenate([-xt[half:], xt[:half]], axis=0)
    return xt * cos + rot * sin


def _pre_kernel(x_ref, g1_ref, wg_ref, wu_ref, wd_ref, gmix_ref, win_t_ref,
                gql_ref, wqu_t_ref, gkvl_ref, wkvu_t_ref,
                gqn_ref, gqr_ref, gkn_ref, gkr_ref, gdq_ref, gdk_ref, cos_ref, sin_ref,
                h1_ref, qm_ref, km_ref, vm_ref, qd_ref, kd_ref, vd_ref, norm_ref,
                *, slope_parts, tiles_per_seq):
    t = x_ref.shape[0]
    x = x_ref[...]
    h1 = x + 0.5 * _swiglu(_rms_rows(x, g1_ref[...]).astype(BF16), wg_ref, wu_ref, wd_ref)
    h1_ref[...] = h1

    hn = _rms_rows(h1, gmix_ref[...]).astype(BF16)
    ut = _dot_nt(win_t_ref[...], hn)
    o_kv = MLA_Q_RANK
    o_kr = o_kv + MLA_KV_RANK
    o_qd = o_kr + MLA_ROPE
    o_kd = o_qd + DIFF_HEADS * 2 * DIFF_QK
    o_vd = o_kd + DIFF_HEADS * 2 * DIFF_QK

    cos = cos_ref[...]
    sin = sin_ref[...]
    ones = jnp.ones((ONES_ROWS, t), BF16)
    zeros_pad = jnp.zeros((QK_PAD - MLA_QK, t), F32)

    q_scale = MLA_QK ** -0.5 * LOG2E
    qln = _rms_cols(ut[0:o_kv], gql_ref[...]).astype(BF16)
    qt = _dot(wqu_t_ref[...], qln)
    kvn = _rms_cols(ut[o_kv:o_kr], gkvl_ref[...]).astype(BF16)
    kvt = _dot(wkvu_t_ref[...], kvn)
    k_rot = _rope_cols(_rms_cols(ut[o_kr:o_qd], gkr_ref[...]), cos, sin)
    for h in range(MLA_HEADS):
        qh = qt[h * MLA_QK:(h + 1) * MLA_QK]
        q_nope = _rms_cols(qh[:MLA_NOPE], gqn_ref[...])
        q_rot = _rope_cols(_rms_cols(qh[MLA_NOPE:], gqr_ref[...]), cos, sin)
        qm_ref[0, h, 0:MLA_NOPE, :] = (q_nope * q_scale).astype(BF16)
        qm_ref[0, h, MLA_NOPE:MLA_QK, :] = (q_rot * q_scale).astype(BF16)
        qm_ref[0, h, MLA_QK:QK_PAD, :] = zeros_pad.astype(BF16)
        kvh = kvt[h * (MLA_NOPE + MLA_V):(h + 1) * (MLA_NOPE + MLA_V)]
        k_nope = _rms_cols(kvh[:MLA_NOPE], gkn_ref[...])
        k_full = jnp.concatenate([k_nope, k_rot, zeros_pad], axis=0)
        km_ref[0, h] = k_full.T.astype(BF16)
        vm_ref[0, h, 0:MLA_V, :] = kvh[MLA_NOPE:].astype(BF16)
        vm_ref[0, h, MLA_V:MLA_V + ONES_ROWS, :] = ones

    d_scale = DIFF_QK ** -0.5 * LOG2E
    zero_q = jnp.zeros((DIFF_QK, t), BF16)
    n_maps = 2 * DIFF_HEADS

    assert t == POS_BLOCK
    blk = (pl.program_id(0) % tiles_per_seq).astype(F32)

    def pick(index, values):
        out = jnp.zeros(index.shape, F32)
        for n, v in enumerate(values):
            out = jnp.where(index == n, v, out)
        return out

    def q_aug(hi, lo):
        r = lax.broadcasted_iota(jnp.int32, (AUG_ROWS, 2 * t), 0)
        rem = (lax.broadcasted_iota(jnp.int32, (AUG_ROWS, 2 * t), 1) % t).astype(F32)
        return pick(r, [blk, blk, rem, rem, POS_BLOCK * hi, POS_BLOCK * lo, hi, lo]).astype(BF16)

    def k_aug(hi, lo):
        width = DIFF_K_WIDTH - 2 * DIFF_QK
        c = lax.broadcasted_iota(jnp.int32, (t, width), 1)
        rem = lax.broadcasted_iota(jnp.int32, (t, width), 0).astype(F32)
        return pick(c, [-POS_BLOCK * hi, -POS_BLOCK * lo, -hi, -lo, blk, blk, rem, rem]).astype(BF16)

    def put_max_sq_norm(row, xt):
        sq = jnp.max(jnp.sum(xt * xt, axis=0, keepdims=True), axis=1, keepdims=True)
        norm_ref[0, row:row + 1, :] = jnp.broadcast_to(sq, (1, norm_ref.shape[2]))

    for h in range(DIFF_HEADS):
        k_maps = []
        for j in range(2):
            r = (2 * h + j) * DIFF_QK
            qj = _rms_cols(ut[o_qd + r:o_qd + r + DIFF_QK], gdq_ref[...]) * d_scale
            kj = _rms_cols(ut[o_kd + r:o_kd + r + DIFF_QK], gdk_ref[...])
            put_max_sq_norm(2 * h + j, qj)
            put_max_sq_norm(n_maps + 2 * h + j, kj)
            qd_ref[0, h, j * DIFF_QK:(j + 1) * DIFF_QK, j * t:(j + 1) * t] = qj.astype(BF16)
            qd_ref[0, h, (1 - j) * DIFF_QK:(2 - j) * DIFF_QK, j * t:(j + 1) * t] = zero_q
            k_maps.append(kj)
        qd_ref[0, h, 2 * DIFF_QK:2 * DIFF_QK + AUG_ROWS, :] = q_aug(*slope_parts[h][1:])
        k12 = jnp.concatenate(k_maps, axis=0)
        kd_ref[0, h, :, 0:2 * DIFF_QK] = k12.T.astype(BF16)
        kd_ref[0, h, :, 2 * DIFF_QK:DIFF_K_WIDTH] = k_aug(*slope_parts[h][1:])
        vd_ref[0, h, 0:DIFF_V, :] = ut[o_vd + h * DIFF_V:o_vd + (h + 1) * DIFF_V].astype(BF16)
        vd_ref[0, h, DIFF_V:DIFF_V + ONES_ROWS, :] = ones


def _key_rows(c, size):
    return pl.ds(pl.multiple_of(c * size, size), size)


def _store_scores(s_buf, max_buf, slot, scores):
    sb = scores.astype(BF16)
    s_buf[slot] = sb
    max_buf[slot] = jnp.max(sb, axis=0, keepdims=True).astype(F32)


def _sweep_head(*, n_tiles, width, window, first_chunk, issue_scores, v_chunk, finalize,
                s_buf, max_buf, p_buf, acc_buf):
    def chunk_of(qi, i, k):
        lo, count = window(qi)
        return lo + (first_chunk(qi) - lo + 2 * i + k) % count

    def step(qi, i, parity, carry, first=False, last=False):
        m, alphas = carry
        cur = [(s_buf[parity, k], max_buf[parity, k]) for k in range(2)]

        nxt_qi, nxt_i = (qi + 1, 0) if last else (qi, i + 1)
        nxt_qi = jnp.minimum(nxt_qi, n_tiles - 1)
        for k in range(2):
            issue_scores(nxt_qi, chunk_of(nxt_qi, nxt_i, k), last and k == 0, (1 - parity, k))

        if first:
            prv_qi = jnp.maximum(qi - 1, 0)
            prv_i = window(prv_qi)[1] // 2 - 1
        else:
            prv_qi, prv_i = qi, i - 1
        for k in range(2):
            acc_buf[...] = (acc_buf[...] * alphas[k]
                            + _dot(v_chunk(chunk_of(prv_qi, prv_i, k)), p_buf[1 - parity, k]))

        if first:
            m = jnp.full((1, width), NEG_BIG, F32)
        alphas = []
        for k, (sb, chunk_max) in enumerate(cur):
            m_new = jnp.maximum(m, chunk_max)
            p_buf[parity, k] = jnp.exp2(sb - m_new.astype(BF16))
            alphas.append(jnp.exp2(m - m_new))
            m = m_new
        return m, tuple(alphas)

    def tile(qi, carry):
        def two_steps(j, cr):
            cr = step(qi, 2 * j + 1, 1, cr)
            return step(qi, 2 * j + 2, 0, cr)
        n_pairs = window(qi)[1] // 2
        carry = lax.fori_loop(0, n_pairs // 2 - 1, two_steps, carry)
        carry = step(qi, n_pairs - 1, 1, carry, last=True)
        carry = step(qi + 1, 0, 0, carry, first=True)
        finalize(qi)
        return carry

    p_buf[...] = jnp.zeros(p_buf.shape, p_buf.dtype)
    acc_buf[...] = jnp.zeros(acc_buf.shape, acc_buf.dtype)
    for k in range(2):
        issue_scores(0, chunk_of(0, 0, k), k == 0, (0, k))
    zero_row = jnp.zeros((1, width), F32)
    carry = (zero_row, (zero_row, zero_row))
    carry = step(0, 0, 0, carry, first=True)
    lax.fori_loop(0, n_tiles, tile, carry)


def _mla_attn_kernel(q_ref, k_ref, v_ref, o_ref, s_buf, max_buf, p_buf, acc_buf):
    chunk, tq = s_buf.shape[2:]
    seq = k_ref.shape[2]

    def tile_cols(qi):
        return pl.ds(pl.multiple_of(qi * tq, tq), tq)

    def issue_scores(qi, c, is_first_chunk, slot):
        _store_scores(s_buf, max_buf, slot,
                      _dot(k_ref[0, 0, _key_rows(c, chunk), :], q_ref[0, 0, :, tile_cols(qi)]))

    def finalize(qi):
        o_ref[0, 0, :, tile_cols(qi)] = (
            acc_buf[0:MLA_V, :] / acc_buf[MLA_V:MLA_V + 1, :]).astype(BF16)

    _sweep_head(n_tiles=seq // tq, width=tq, window=lambda qi: (0, seq // chunk),
                first_chunk=lambda qi: 0, issue_scores=issue_scores,
                v_chunk=lambda c: v_ref[0, 0, :, _key_rows(c, chunk)], finalize=finalize,
                s_buf=s_buf, max_buf=max_buf, p_buf=p_buf, acc_buf=acc_buf)


def _diff_attn_kernel(reach_ref, q_ref, k_ref, v_ref, lq1_ref, lk1_ref, lq2_ref, lk2_ref, gsub_ref,
                      o_ref, s_buf, max_buf, p_buf, acc_buf, *, lambda_init):
    chunk, tq = s_buf.shape[2], s_buf.shape[3] // 2
    seq = k_ref.shape[2]
    n_chunks = seq // chunk
    assert chunk % tq == 0 and n_chunks % 4 == 0
    first_chunk = lambda qi: (qi * tq) // chunk
    reach = reach_ref[pl.program_id(0) * pl.num_programs(1) + pl.program_id(1)]

    def window(qi):
        lo = jnp.maximum(qi * tq - reach, 0) // chunk
        hi = jnp.minimum(qi * tq + tq - 1 + reach, seq - 1) // chunk
        count = jnp.minimum((hi - lo + 4) // 4 * 4, n_chunks)
        return jnp.minimum(lo, n_chunks - count), count

    lam = (jnp.exp(jnp.sum(lq1_ref[...] * lk1_ref[...], axis=-1, keepdims=True))
           - jnp.exp(jnp.sum(lq2_ref[...] * lk2_ref[...], axis=-1, keepdims=True)) + lambda_init)
    zero_rows = jnp.zeros((DIFF_K_WIDTH - 2 * DIFF_QK - AUG_ROWS, 2 * tq), BF16)

    def issue_scores(qi, c, is_first_chunk, slot):
        qa = q_ref[0, 0, :, pl.ds(pl.multiple_of(qi * 2 * tq, 2 * tq), 2 * tq)]
        kc = k_ref[0, 0, _key_rows(c, chunk), :]

        def scores(sign):
            aug = (qa[2 * DIFF_QK:].astype(F32) * sign).astype(BF16)
            return _dot(kc, jnp.concatenate([qa[:2 * DIFF_QK], aug, zero_rows], axis=0))

        if is_first_chunk:
            st = jnp.minimum(scores(1.0), scores(-1.0))
        else:
            st = scores(jnp.where(c < first_chunk(qi), 1.0, -1.0).astype(F32))
        _store_scores(s_buf, max_buf, slot, st)

    def finalize(qi):
        o_all = acc_buf[0:DIFF_V, :] / acc_buf[DIFF_V:DIFF_V + 1, :]
        sub = POS_BLOCK
        o = jnp.concatenate(
            [o_all[:, 2 * j * sub:(2 * j + 1) * sub] - lam * o_all[:, (2 * j + 1) * sub:(2 * j + 2) * sub]
             for j in range(tq // sub)], axis=1)
        o_ref[0, 0, :, pl.ds(pl.multiple_of(qi * tq, tq), tq)] = (
            _rms_cols(o, gsub_ref[...]) * (1.0 - lambda_init)).astype(BF16)

    _sweep_head(n_tiles=seq // tq, width=2 * tq, window=window,
                first_chunk=first_chunk, issue_scores=issue_scores,
                v_chunk=lambda c: v_ref[0, 0, :, _key_rows(c, chunk)], finalize=finalize,
                s_buf=s_buf, max_buf=max_buf, p_buf=p_buf, acc_buf=acc_buf)


def _post_kernel(h1_ref, om_ref, od_ref, p_ref, wom_ref, wod_ref, g2_ref, wg_ref, wu_ref, wd_ref,
                 gpi_ref, wpg_ref, bpg_ref, wpp_ref, gpo_ref, out_ref):
    h = h1_ref[...] + _dot_tn(om_ref[0], wom_ref[...]) + _dot_tn(od_ref[0], wod_ref[...])
    h = h + 0.5 * _swiglu(_rms_rows(h, g2_ref[...]).astype(BF16), wg_ref, wu_ref, wd_ref)
    z = _dot(_rms_rows(h, gpi_ref[...]).astype(BF16), wpg_ref[...]) + bpg_ref[...]
    gate = 1.0 / (1.0 + jnp.exp(-z))
    emb = _rms_rows(_dot(p_ref[...].astype(BF16), wpp_ref[...]), gpo_ref[...])
    out_ref[...] = h + gate * emb


def _resident(shape):
    nd = len(shape)
    return pl.BlockSpec(shape, lambda *_: (0,) * nd, pipeline_mode=pl.Buffered(1))


def _params(n_axes):
    return pltpu.CompilerParams(dimension_semantics=("arbitrary",) * n_axes,
                                vmem_limit_bytes=VMEM_LIMIT_BYTES)


def _attn_scratch(chunk, width, rows):
    return [pltpu.VMEM((2, 2, chunk, width), BF16), pltpu.VMEM((2, 2, 1, width), F32),
            pltpu.VMEM((2, 2, chunk, width), BF16), pltpu.VMEM((rows, width), F32)]


def _rope_tables_t(seq):
    pos = jnp.arange(seq, dtype=F32)
    inv = ROPE_THETA ** (-jnp.arange(0, MLA_ROPE, 2, dtype=F32) / MLA_ROPE)
    ang = pos[:, None] * inv[None, :]
    ang = jnp.concatenate([ang, ang], axis=-1)
    return jnp.cos(ang).T, jnp.sin(ang).T


def _alibi_reach(sq_norms, slopes2, batch, seq):
    n_maps = 2 * DIFF_HEADS
    top = jnp.max(sq_norms[:, :, 0].reshape(batch, -1, 2 * n_maps), axis=1)
    bound = jnp.sqrt(jnp.max((top[:, :n_maps] * top[:, n_maps:]).reshape(batch, DIFF_HEADS, 2), axis=-1))
    bound = bound * 1.02
    reach = jnp.ceil((2.0 * bound + WEIGHT_CUTOFF_BITS) / slopes2[None, :])
    return jnp.minimum(reach, float(seq)).astype(jnp.int32).reshape(-1)


def _alibi_slope_parts():
    parts = []
    for h in range(DIFF_HEADS):
        s2 = np.float32(LOG2E * 2.0 ** (-8.0 * (h + 1) / DIFF_HEADS))
        bits = s2.view(np.uint32)
        hi = np.uint32((bits + np.uint32(0x7FFF) + ((bits >> np.uint32(16)) & np.uint32(1)))
                       & np.uint32(0xFFFF0000)).view(np.float32)
        parts.append((float(s2), float(hi), float(s2 - hi)))
    return tuple(parts)


def kernel(x, p, g_ffn1, w_ffn1_gate, w_ffn1_up, w_ffn1_down, g_mix, w_in, g_q_lat, w_q_up, g_kv_lat, w_kv_up, g_mla_q, g_mla_k, g_diff_q, g_diff_k, lambda_q1, lambda_k1, lambda_q2, lambda_k2, g_diff_sub, w_out, g_ffn2, w_ffn2_gate, w_ffn2_up, w_ffn2_down, g_ple_in, w_ple_gate, b_ple_gate, w_ple_proj, g_ple_out):
    batch, seq, d_model = x.shape
    depth = p.shape[0]
    tokens = batch * seq
    tm = TOKEN_TILE
    tiles_per_seq = seq // tm
    assert seq % tm == 0 and seq % POST_TOKEN_TILE == 0 and seq % MLA_Q_TILE == 0
    assert seq % (4 * MLA_KV_CHUNK) == 0 and seq % (4 * DIFF_KV_CHUNK) == 0
    assert tm == POS_BLOCK and DIFF_Q_TILE % POS_BLOCK == 0 and seq % DIFF_Q_TILE == 0

    cos_t, sin_t = _rope_tables_t(seq)
    slope_parts = _alibi_slope_parts()
    slopes2 = jnp.asarray([s2 for s2, _, _ in slope_parts], F32)
    row = lambda v: v.reshape(1, -1)
    col = lambda v: v.reshape(-1, 1)
    bf = lambda w: w.astype(BF16)

    h = x.reshape(tokens, d_model)
    for i in range(depth):
        lambda_init = 0.8 - 0.6 * math.exp(-0.3 * i)
        ple_dim = p.shape[-1]

        tok_spec = pl.BlockSpec((tm, d_model), lambda t: (t, 0))
        head_t = lambda heads, rows, width: pl.BlockSpec(
            (1, heads, rows, width), lambda t: (t // tiles_per_seq, 0, 0, t % tiles_per_seq))
        head_r = lambda heads, width: pl.BlockSpec(
            (1, heads, tm, width), lambda t: (t // tiles_per_seq, 0, t % tiles_per_seq, 0))
        rope_spec = pl.BlockSpec((MLA_ROPE, tm), lambda t: (0, t % tiles_per_seq))
        pre_inputs = [
            (h, tok_spec),
            (row(g_ffn1[i]), None), (bf(w_ffn1_gate[i]), None), (bf(w_ffn1_up[i]), None),
            (bf(w_ffn1_down[i]), None), (row(g_mix[i]), None), (bf(w_in[i].T), None),
            (col(g_q_lat[i]), None), (bf(w_q_up[i].T), None),
            (col(g_kv_lat[i]), None), (bf(w_kv_up[i].T), None),
            (col(g_mla_q[i, :MLA_NOPE]), None), (col(g_mla_q[i, MLA_NOPE:]), None),
            (col(g_mla_k[i, :MLA_NOPE]), None), (col(g_mla_k[i, MLA_NOPE:]), None),
            (col(g_diff_q[i]), None), (col(g_diff_k[i]), None),
            (cos_t, rope_spec), (sin_t, rope_spec),
        ]
        dq_rows = 2 * DIFF_QK + AUG_ROWS
        h1, qm, km, vm, qd, kd, vd, sq_norms = pl.pallas_call(
            functools.partial(_pre_kernel, slope_parts=slope_parts, tiles_per_seq=tiles_per_seq),
            grid=(tokens // tm,),
            in_specs=[s if s is not None else _resident(a.shape) for a, s in pre_inputs],
            out_specs=[tok_spec, head_t(MLA_HEADS, QK_PAD, tm), head_r(MLA_HEADS, QK_PAD),
                       head_t(MLA_HEADS, MLA_V + ONES_ROWS, tm), head_t(DIFF_HEADS, dq_rows, 2 * tm),
                       head_r(DIFF_HEADS, DIFF_K_WIDTH), head_t(DIFF_HEADS, DIFF_V + ONES_ROWS, tm),
                       pl.BlockSpec((1, 4 * DIFF_HEADS, LANES), lambda t: (t, 0, 0))],
            out_shape=[
                jax.ShapeDtypeStruct((tokens, d_model), F32),
                jax.ShapeDtypeStruct((batch, MLA_HEADS, QK_PAD, seq), BF16),
                jax.ShapeDtypeStruct((batch, MLA_HEADS, seq, QK_PAD), BF16),
                jax.ShapeDtypeStruct((batch, MLA_HEADS, MLA_V + ONES_ROWS, seq), BF16),
                jax.ShapeDtypeStruct((batch, DIFF_HEADS, dq_rows, 2 * seq), BF16),
                jax.ShapeDtypeStruct((batch, DIFF_HEADS, seq, DIFF_K_WIDTH), BF16),
                jax.ShapeDtypeStruct((batch, DIFF_HEADS, DIFF_V + ONES_ROWS, seq), BF16),
                jax.ShapeDtypeStruct((tokens // tm, 4 * DIFF_HEADS, LANES), F32),
            ],
            compiler_params=_params(1),
            name="pre",
        )(*[a for a, _ in pre_inputs])

        per_head = lambda *tail: pl.BlockSpec((1, 1) + tail, lambda b, hh: (b, hh, 0, 0))
        om = pl.pallas_call(
            _mla_attn_kernel,
            grid=(batch, MLA_HEADS),
            in_specs=[per_head(QK_PAD, seq), per_head(seq, QK_PAD), per_head(MLA_V + ONES_ROWS, seq)],
            out_specs=per_head(MLA_V, seq),
            out_shape=jax.ShapeDtypeStruct((batch, MLA_HEADS, MLA_V, seq), BF16),
            scratch_shapes=_attn_scratch(MLA_KV_CHUNK, MLA_Q_TILE, MLA_V + ONES_ROWS),
            compiler_params=_params(2),
            name="mla_attn",
        )(qm, km, vm)

        small = lambda shape: pl.BlockSpec(shape, lambda b, hh: (0,) * len(shape))
        od = pl.pallas_call(
            functools.partial(_diff_attn_kernel, lambda_init=lambda_init),
            grid=(batch, DIFF_HEADS),
            in_specs=[pl.BlockSpec(memory_space=pltpu.SMEM),
                      per_head(dq_rows, 2 * seq), per_head(seq, DIFF_K_WIDTH),
                      per_head(DIFF_V + ONES_ROWS, seq)]
                     + [small((1, DIFF_QK))] * 4 + [small((DIFF_V, 1))],
            out_specs=per_head(DIFF_V, seq),
            out_shape=jax.ShapeDtypeStruct((batch, DIFF_HEADS, DIFF_V, seq), BF16),
            scratch_shapes=_attn_scratch(DIFF_KV_CHUNK, 2 * DIFF_Q_TILE, DIFF_V + ONES_ROWS),
            compiler_params=_params(2),
            name="diff_attn",
        )(_alibi_reach(sq_norms, slopes2, batch, seq), qd, kd, vd, row(lambda_q1[i]), row(lambda_k1[i]), row(lambda_q2[i]),
          row(lambda_k2[i]), col(g_diff_sub[i]))

        mla_w = MLA_HEADS * MLA_V
        diff_w = DIFF_HEADS * DIFF_V
        tp = POST_TOKEN_TILE
        post_tiles_per_seq = seq // tp
        post_tok_spec = pl.BlockSpec((tp, d_model), lambda t: (t, 0))
        feat_t = lambda width: pl.BlockSpec(
            (1, width, tp), lambda t: (t // post_tiles_per_seq, 0, t % post_tiles_per_seq))
        post_inputs = [
            (h1, post_tok_spec),
            (om.reshape(batch, mla_w, seq), feat_t(mla_w)),
            (od.reshape(batch, diff_w, seq), feat_t(diff_w)),
            (p[i].reshape(tokens, ple_dim), pl.BlockSpec((tp, ple_dim), lambda t: (t, 0))),
            (bf(w_out[i, :mla_w]), None), (bf(w_out[i, mla_w:]), None),
            (row(g_ffn2[i]), None), (bf(w_ffn2_gate[i]), None), (bf(w_ffn2_up[i]), None),
            (bf(w_ffn2_down[i]), None),
            (row(g_ple_in[i]), None), (bf(w_ple_gate[i]), None), (row(b_ple_gate[i]), None),
            (bf(w_ple_proj[i]), None), (row(g_ple_out[i]), None),
        ]
        h = pl.pallas_call(
            _post_kernel,
            grid=(tokens // tp,),
            in_specs=[s if s is not None else _resident(a.shape) for a, s in post_inputs],
            out_specs=post_tok_spec,
            out_shape=jax.ShapeDtypeStruct((tokens, d_model), F32),
            compiler_params=_params(1),
            name="post",
        )(*[a for a, _ in post_inputs])

    return h.reshape(batch, seq, d_model)
```

```python
import functools
import math

import jax
import jax.numpy as jnp
import numpy as np
from jax import lax
from jax.experimental import pallas as pl
from jax.experimental.pallas import tpu as pltpu

F32 = jnp.float32
BF16 = jnp.bfloat16

MLA_HEADS = 8
MLA_NOPE = 64
MLA_ROPE = 32
MLA_QK = MLA_NOPE + MLA_ROPE
MLA_V = 64
MLA_Q_RANK = 256
MLA_KV_RANK = 128
ROPE_THETA = 10000.0
DIFF_HEADS = 4
DIFF_QK = 64
DIFF_V = 2 * DIFF_QK
EPS = 1e-6

QK_PAD = 128
ONES_ROWS = 16
AUG_ROWS = 16
DIFF_K_WIDTH = 256
POS_BLOCK = 256
NEG_BIG = -1e30
WEIGHT_CUTOFF_BITS = 130.0
LANES = 128
LOG2E = math.log2(math.e)

VMEM_LIMIT_BYTES = 56 * 1024 * 1024
TOKEN_TILE = 256
POST_TOKEN_TILE = 512
MLA_Q_TILE = 1024
DIFF_Q_TILE = 512
MLA_KV_CHUNK = 512
DIFF_KV_CHUNK = 512


def _dot(a, b):
    return jnp.dot(a, b, preferred_element_type=F32)


def _dot_nt(a, b):
    return lax.dot_general(a, b, (((1,), (1,)), ((), ())), preferred_element_type=F32)


def _dot_tn(a, b):
    return lax.dot_general(a, b, (((0,), (0,)), ((), ())), preferred_element_type=F32)


def _rms_rows(x, g_row):
    ms = jnp.mean(x * x, axis=-1, keepdims=True)
    return x * lax.rsqrt(ms + EPS) * g_row


def _rms_cols(xt, g_col):
    ms = jnp.mean(xt * xt, axis=0, keepdims=True)
    return xt * lax.rsqrt(ms + EPS) * g_col


def _swiglu(xn, wg_ref, wu_ref, wd_ref):
    gate = _dot(xn, wg_ref[...])
    up = _dot(xn, wu_ref[...])
    act = (gate / (1.0 + jnp.exp(-gate))) * up
    return _dot(act.astype(BF16), wd_ref[...])


def _rope_cols(xt, cos, sin):
    half = MLA_ROPE // 2
    rot = jnp.concatenate([-xt[half:], xt[:half]], axis=0)
    return xt * cos + rot * sin


def _pre_kernel(x_ref, g1_ref, wg_ref, wu_ref, wd_ref, gmix_ref, win_t_ref,
                gql_ref, wqu_t_ref, gkvl_ref, wkvu_t_ref,
                gqn_ref, gqr_ref, gkn_ref, gkr_ref, gdq_ref, gdk_ref, cos_ref, sin_ref,
                h1_ref, qm_ref, km_ref, vm_ref, qd_ref, kd_ref, vd_ref, norm_ref,
                *, slope_parts, tiles_per_seq):
    t = x_ref.shape[0]
    x = x_ref[...]
    h1 = x + 0.5 * _swiglu(_rms_rows(x, g1_ref[...]).astype(BF16), wg_ref, wu_ref, wd_ref)
    h1_ref[...] = h1

    hn = _rms_rows(h1, gmix_ref[...]).astype(BF16)
    ut = _dot_nt(win_t_ref[...], hn)
    o_kv = MLA_Q_RANK
    o_kr = o_kv + MLA_KV_RANK
    o_qd = o_kr + MLA_ROPE
    o_kd = o_qd + DIFF_HEADS * 2 * DIFF_QK
    o_vd = o_kd + DIFF_HEADS * 2 * DIFF_QK

    cos = cos_ref[...]
    sin = sin_ref[...]
    ones = jnp.ones((ONES_ROWS, t), BF16)
    zeros_pad = jnp.zeros((QK_PAD - MLA_QK, t), F32)

    q_scale = MLA_QK ** -0.5 * LOG2E
    qln = _rms_cols(ut[0:o_kv], gql_ref[...]).astype(BF16)
    qt = _dot(wqu_t_ref[...], qln)
    kvn = _rms_cols(ut[o_kv:o_kr], gkvl_ref[...]).astype(BF16)
    kvt = _dot(wkvu_t_ref[...], kvn)
    k_rot = _rope_cols(_rms_cols(ut[o_kr:o_qd], gkr_ref[...]), cos, sin)
    for h in range(MLA_HEADS):
        qh = qt[h * MLA_QK:(h + 1) * MLA_QK]
        q_nope = _rms_cols(qh[:MLA_NOPE], gqn_ref[...])
        q_rot = _rope_cols(_rms_cols(qh[MLA_NOPE:], gqr_ref[...]), cos, sin)
        qm_ref[0, h, 0:MLA_NOPE, :] = (q_nope * q_scale).astype(BF16)
        qm_ref[0, h, MLA_NOPE:MLA_QK, :] = (q_rot * q_scale).astype(BF16)
        qm_ref[0, h, MLA_QK:QK_PAD, :] = zeros_pad.astype(BF16)
        kvh = kvt[h * (MLA_NOPE + MLA_V):(h + 1) * (MLA_NOPE + MLA_V)]
        k_nope = _rms_cols(kvh[:MLA_NOPE], gkn_ref[...])
        k_full = jnp.concatenate([k_nope, k_rot, zeros_pad], axis=0)
        km_ref[0, h] = k_full.T.astype(BF16)
        vm_ref[0, h, 0:MLA_V, :] = kvh[MLA_NOPE:].astype(BF16)
        vm_ref[0, h, MLA_V:MLA_V + ONES_ROWS, :] = ones

    d_scale = DIFF_QK ** -0.5 * LOG2E
    zero_q = jnp.zeros((DIFF_QK, t), BF16)
    n_maps = 2 * DIFF_HEADS

    assert t == POS_BLOCK
    blk = (pl.program_id(0) % tiles_per_seq).astype(F32)

    def pick(index, values):
        out = jnp.zeros(index.shape, F32)
        for n, v in enumerate(values):
            out = jnp.where(index == n, v, out)
        return out

    def q_aug(hi, lo):
        r = lax.broadcasted_iota(jnp.int32, (AUG_ROWS, 2 * t), 0)
        rem = (lax.broadcasted_iota(jnp.int32, (AUG_ROWS, 2 * t), 1) % t).astype(F32)
        return pick(r, [blk, blk, rem, rem, POS_BLOCK * hi, POS_BLOCK * lo, hi, lo]).astype(BF16)

    def k_aug(hi, lo):
        width = DIFF_K_WIDTH - 2 * DIFF_QK
        c = lax.broadcasted_iota(jnp.int32, (t, width), 1)
        rem = lax.broadcasted_iota(jnp.int32, (t, width), 0).astype(F32)
        return pick(c, [-POS_BLOCK * hi, -POS_BLOCK * lo, -hi, -lo, blk, blk, rem, rem]).astype(BF16)

    def put_max_sq_norm(row, xt):
        sq = jnp.max(jnp.sum(xt * xt, axis=0, keepdims=True), axis=1, keepdims=True)
        norm_ref[0, row:row + 1, :] = jnp.broadcast_to(sq, (1, norm_ref.shape[2]))

    for h in range(DIFF_HEADS):
        k_maps = []
        for j in range(2):
            r = (2 * h + j) * DIFF_QK
            qj = _rms_cols(ut[o_qd + r:o_qd + r + DIFF_QK], gdq_ref[...]) * d_scale
            kj = _rms_cols(ut[o_kd + r:o_kd + r + DIFF_QK], gdk_ref[...])
            put_max_sq_norm(2 * h + j, qj)
            put_max_sq_norm(n_maps + 2 * h + j, kj)
            qd_ref[0, h, j * DIFF_QK:(j + 1) * DIFF_QK, j * t:(j + 1) * t] = qj.astype(BF16)
            qd_ref[0, h, (1 - j) * DIFF_QK:(2 - j) * DIFF_QK, j * t:(j + 1) * t] = zero_q
            k_maps.append(kj)
        qd_ref[0, h, 2 * DIFF_QK:2 * DIFF_QK + AUG_ROWS, :] = q_aug(*slope_parts[h][1:])
        k12 = jnp.concatenate(k_maps, axis=0)
        kd_ref[0, h, :, 0:2 * DIFF_QK] = k12.T.astype(BF16)
        kd_ref[0, h, :, 2 * DIFF_QK:DIFF_K_WIDTH] = k_aug(*slope_parts[h][1:])
        vd_ref[0, h, 0:DIFF_V, :] = ut[o_vd + h * DIFF_V:o_vd + (h + 1) * DIFF_V].astype(BF16)
        vd_ref[0, h, DIFF_V:DIFF_V + ONES_ROWS, :] = ones


def _key_rows(c, size):
    return pl.ds(pl.multiple_of(c * size, size), size)


def _store_scores(s_buf, max_buf, slot, scores):
    s_buf[slot] = scores.astype(BF16)
    max_buf[slot] = jnp.max(scores, axis=0, keepdims=True).astype(BF16).astype(F32)


def _sweep_head(*, n_tiles, width, window, first_chunk, issue_scores, v_chunk, finalize,
                s_buf, max_buf, p_buf, acc_buf):
    def chunk_of(qi, i, k):
        lo, count = window(qi)
        return lo + (first_chunk(qi) - lo + 2 * i + k) % count

    def step(qi, i, parity, carry, first=False, last=False):
        m, alphas = carry
        cur = [(s_buf[parity, k], max_buf[parity, k]) for k in range(2)]

        nxt_qi, nxt_i = (qi + 1, 0) if last else (qi, i + 1)
        nxt_qi = jnp.minimum(nxt_qi, n_tiles - 1)
        for k in range(2):
            issue_scores(nxt_qi, chunk_of(nxt_qi, nxt_i, k), last and k == 0, (1 - parity, k))

        if first:
            prv_qi = jnp.maximum(qi - 1, 0)
            prv_i = window(prv_qi)[1] // 2 - 1
        else:
            prv_qi, prv_i = qi, i - 1
        for k in range(2):
            acc_buf[...] = (acc_buf[...] * alphas[k]
                            + _dot(v_chunk(chunk_of(prv_qi, prv_i, k)), p_buf[1 - parity, k]))

        if first:
            m = jnp.full((1, width), NEG_BIG, F32)
        alphas = []
        for k, (sb, chunk_max) in enumerate(cur):
            m_new = jnp.maximum(m, chunk_max)
            p_buf[parity, k] = jnp.exp2(sb - m_new.astype(BF16))
            alphas.append(jnp.exp2(m - m_new))
            m = m_new
        return m, tuple(alphas)

    def tile(qi, carry):
        def two_steps(j, cr):
            cr = step(qi, 2 * j + 1, 1, cr)
            return step(qi, 2 * j + 2, 0, cr)
        n_pairs = window(qi)[1] // 2
        carry = lax.fori_loop(0, n_pairs // 2 - 1, two_steps, carry)
        carry = step(qi, n_pairs - 1, 1, carry, last=True)
        carry = step(qi + 1, 0, 0, carry, first=True)
        finalize(qi)
        return carry

    p_buf[...] = jnp.zeros(p_buf.shape, p_buf.dtype)
    acc_buf[...] = jnp.zeros(acc_buf.shape, acc_buf.dtype)
    for k in range(2):
        issue_scores(0, chunk_of(0, 0, k), k == 0, (0, k))
    zero_row = jnp.zeros((1, width), F32)
    carry = (zero_row, (zero_row, zero_row))
    carry = step(0, 0, 0, carry, first=True)
    lax.fori_loop(0, n_tiles, tile, carry)


def _mla_attn_kernel(q_ref, k_ref, v_ref, o_ref, s_buf, max_buf, p_buf, acc_buf):
    chunk, tq = s_buf.shape[2:]
    seq = k_ref.shape[2]

    def tile_cols(qi):
        return pl.ds(pl.multiple_of(qi * tq, tq), tq)

    def issue_scores(qi, c, is_first_chunk, slot):
        _store_scores(s_buf, max_buf, slot,
                      _dot(k_ref[0, 0, _key_rows(c, chunk), :], q_ref[0, 0, :, tile_cols(qi)]))

    def finalize(qi):
        o_ref[0, 0, :, tile_cols(qi)] = (
            acc_buf[0:MLA_V, :] / acc_buf[MLA_V:MLA_V + 1, :]).astype(BF16)

    _sweep_head(n_tiles=seq // tq, width=tq, window=lambda qi: (0, seq // chunk),
                first_chunk=lambda qi: 0, issue_scores=issue_scores,
                v_chunk=lambda c: v_ref[0, 0, :, _key_rows(c, chunk)], finalize=finalize,
                s_buf=s_buf, max_buf=max_buf, p_buf=p_buf, acc_buf=acc_buf)


def _diff_attn_kernel(reach_ref, q_ref, k_ref, v_ref, lq1_ref, lk1_ref, lq2_ref, lk2_ref, gsub_ref,
                      o_ref, s_buf, max_buf, p_buf, acc_buf, *, lambda_init):
    chunk, tq = s_buf.shape[2], s_buf.shape[3] // 2
    seq = k_ref.shape[2]
    n_chunks = seq // chunk
    assert chunk % tq == 0 and n_chunks % 4 == 0
    first_chunk = lambda qi: (qi * tq) // chunk
    reach = reach_ref[pl.program_id(0) * pl.num_programs(1) + pl.program_id(1)]

    def window(qi):
        lo = jnp.maximum(qi * tq - reach, 0) // chunk
        hi = jnp.minimum(qi * tq + tq - 1 + reach, seq - 1) // chunk
        count = jnp.minimum((hi - lo + 4) // 4 * 4, n_chunks)
        return jnp.minimum(lo, n_chunks - count), count

    lam = (jnp.exp(jnp.sum(lq1_ref[...] * lk1_ref[...], axis=-1, keepdims=True))
           - jnp.exp(jnp.sum(lq2_ref[...] * lk2_ref[...], axis=-1, keepdims=True)) + lambda_init)
    zero_rows = jnp.zeros((DIFF_K_WIDTH - 2 * DIFF_QK - AUG_ROWS, 2 * tq), BF16)

    def issue_scores(qi, c, is_first_chunk, slot):
        qa = q_ref[0, 0, :, pl.ds(pl.multiple_of(qi * 2 * tq, 2 * tq), 2 * tq)]
        kc = k_ref[0, 0, _key_rows(c, chunk), :]

        def scores(sign):
            aug = (qa[2 * DIFF_QK:].astype(F32) * sign).astype(BF16)
            return _dot(kc, jnp.concatenate([qa[:2 * DIFF_QK], aug, zero_rows], axis=0))

        if is_first_chunk:
            st = jnp.minimum(scores(1.0), scores(-1.0))
        else:
            st = scores(jnp.where(c < first_chunk(qi), 1.0, -1.0).astype(F32))
        _store_scores(s_buf, max_buf, slot, st)

    def finalize(qi):
        o_all = acc_buf[0:DIFF_V, :] / acc_buf[DIFF_V:DIFF_V + 1, :]
        sub = POS_BLOCK
        o = jnp.concatenate(
            [o_all[:, 2 * j * sub:(2 * j + 1) * sub] - lam * o_all[:, (2 * j + 1) * sub:(2 * j + 2) * sub]
             for j in range(tq // sub)], axis=1)
        o_ref[0, 0, :, pl.ds(pl.multiple_of(qi * tq, tq), tq)] = (
            _rms_cols(o, gsub_ref[...]) * (1.0 - lambda_init)).astype(BF16)

    _sweep_head(n_tiles=seq // tq, width=2 * tq, window=window,
                first_chunk=first_chunk, issue_scores=issue_scores,
                v_chunk=lambda c: v_ref[0, 0, :, _key_rows(c, chunk)], finalize=finalize,
                s_buf=s_buf, max_buf=max_buf, p_buf=p_buf, acc_buf=acc_buf)


def _post_kernel(h1_ref, om_ref, od_ref, p_ref, wom_ref, wod_ref, g2_ref, wg_ref, wu_ref, wd_ref,
                 gpi_ref, wpg_ref, bpg_ref, wpp_ref, gpo_ref, out_ref):
    h = h1_ref[...] + _dot_tn(om_ref[0], wom_ref[...]) + _dot_tn(od_ref[0], wod_ref[...])
    h = h + 0.5 * _swiglu(_rms_rows(h, g2_ref[...]).astype(BF16), wg_ref, wu_ref, wd_ref)
    z = _dot(_rms_rows(h, gpi_ref[...]).astype(BF16), wpg_ref[...]) + bpg_ref[...]
    gate = 1.0 / (1.0 + jnp.exp(-z))
    emb = _rms_rows(_dot(p_ref[...].astype(BF16), wpp_ref[...]), gpo_ref[...])
    out_ref[...] = h + gate * emb


def _resident(shape):
    nd = len(shape)
    return pl.BlockSpec(shape, lambda *_: (0,) * nd, pipeline_mode=pl.Buffered(1))


def _params(n_axes):
    return pltpu.CompilerParams(dimension_semantics=("arbitrary",) * n_axes,
                                vmem_limit_bytes=VMEM_LIMIT_BYTES)


def _attn_scratch(chunk, width, rows):
    return [pltpu.VMEM((2, 2, chunk, width), BF16), pltpu.VMEM((2, 2, 1, width), F32),
            pltpu.VMEM((2, 2, chunk, width), BF16), pltpu.VMEM((rows, width), F32)]


def _rope_tables_t(seq):
    pos = jnp.arange(seq, dtype=F32)
    inv = ROPE_THETA ** (-jnp.arange(0, MLA_ROPE, 2, dtype=F32) / MLA_ROPE)
    ang = pos[:, None] * inv[None, :]
    ang = jnp.concatenate([ang, ang], axis=-1)
    return jnp.cos(ang).T, jnp.sin(ang).T


def _alibi_reach(sq_norms, slopes2, batch, seq):
    n_maps = 2 * DIFF_HEADS
    top = jnp.max(sq_norms[:, :, 0].reshape(batch, -1, 2 * n_maps), axis=1)
    bound = jnp.sqrt(jnp.max((top[:, :n_maps] * top[:, n_maps:]).reshape(batch, DIFF_HEADS, 2), axis=-1))
    bound = bound * 1.02
    reach = jnp.ceil((2.0 * bound + WEIGHT_CUTOFF_BITS) / slopes2[None, :])
    return jnp.minimum(reach, float(seq)).astype(jnp.int32).reshape(-1)


def _alibi_slope_parts():
    parts = []
    for h in range(DIFF_HEADS):
        s2 = np.float32(LOG2E * 2.0 ** (-8.0 * (h + 1) / DIFF_HEADS))
        bits = s2.view(np.uint32)
        hi = np.uint32((bits + np.uint32(0x7FFF) + ((bits >> np.uint32(16)) & np.uint32(1)))
                       & np.uint32(0xFFFF0000)).view(np.float32)
        parts.append((float(s2), float(hi), float(s2 - hi)))
    return tuple(parts)


def kernel(x, p, g_ffn1, w_ffn1_gate, w_ffn1_up, w_ffn1_down, g_mix, w_in, g_q_lat, w_q_up, g_kv_lat, w_kv_up, g_mla_q, g_mla_k, g_diff_q, g_diff_k, lambda_q1, lambda_k1, lambda_q2, lambda_k2, g_diff_sub, w_out, g_ffn2, w_ffn2_gate, w_ffn2_up, w_ffn2_down, g_ple_in, w_ple_gate, b_ple_gate, w_ple_proj, g_ple_out):
    batch, seq, d_model = x.shape
    depth = p.shape[0]
    tokens = batch * seq
    tm = TOKEN_TILE
    tiles_per_seq = seq // tm
    assert seq % tm == 0 and seq % POST_TOKEN_TILE == 0 and seq % MLA_Q_TILE == 0
    assert seq % (4 * MLA_KV_CHUNK) == 0 and seq % (4 * DIFF_KV_CHUNK) == 0
    assert tm == POS_BLOCK and DIFF_Q_TILE % POS_BLOCK == 0 and seq % DIFF_Q_TILE == 0

    cos_t, sin_t = _rope_tables_t(seq)
    slope_parts = _alibi_slope_parts()
    slopes2 = jnp.asarray([s2 for s2, _, _ in slope_parts], F32)
    row = lambda v: v.reshape(1, -1)
    col = lambda v: v.reshape(-1, 1)
    bf = lambda w: w.astype(BF16)

    h = x.reshape(tokens, d_model)
    for i in range(depth):
        lambda_init = 0.8 - 0.6 * math.exp(-0.3 * i)
        ple_dim = p.shape[-1]

        tok_spec = pl.BlockSpec((tm, d_model), lambda t: (t, 0))
        head_t = lambda heads, rows, width: pl.BlockSpec(
            (1, heads, rows, width), lambda t: (t // tiles_per_seq, 0, 0, t % tiles_per_seq))
        head_r = lambda heads, width: pl.BlockSpec(
            (1, heads, tm, width), lambda t: (t // tiles_per_seq, 0, t % tiles_per_seq, 0))
        rope_spec = pl.BlockSpec((MLA_ROPE, tm), lambda t: (0, t % tiles_per_seq))
        pre_inputs = [
            (h, tok_spec),
            (row(g_ffn1[i]), None), (bf(w_ffn1_gate[i]), None), (bf(w_ffn1_up[i]), None),
            (bf(w_ffn1_down[i]), None), (row(g_mix[i]), None), (bf(w_in[i].T), None),
            (col(g_q_lat[i]), None), (bf(w_q_up[i].T), None),
            (col(g_kv_lat[i]), None), (bf(w_kv_up[i].T), None),
            (col(g_mla_q[i, :MLA_NOPE]), None), (col(g_mla_q[i, MLA_NOPE:]), None),
            (col(g_mla_k[i, :MLA_NOPE]), None), (col(g_mla_k[i, MLA_NOPE:]), None),
            (col(g_diff_q[i]), None), (col(g_diff_k[i]), None),
            (cos_t, rope_spec), (sin_t, rope_spec),
        ]
        dq_rows = 2 * DIFF_QK + AUG_ROWS
        h1, qm, km, vm, qd, kd, vd, sq_norms = pl.pallas_call(
            functools.partial(_pre_kernel, slope_parts=slope_parts, tiles_per_seq=tiles_per_seq),
            grid=(tokens // tm,),
            in_specs=[s if s is not None else _resident(a.shape) for a, s in pre_inputs],
            out_specs=[tok_spec, head_t(MLA_HEADS, QK_PAD, tm), head_r(MLA_HEADS, QK_PAD),
                       head_t(MLA_HEADS, MLA_V + ONES_ROWS, tm), head_t(DIFF_HEADS, dq_rows, 2 * tm),
                       head_r(DIFF_HEADS, DIFF_K_WIDTH), head_t(DIFF_HEADS, DIFF_V + ONES_ROWS, tm),
                       pl.BlockSpec((1, 4 * DIFF_HEADS, LANES), lambda t: (t, 0, 0))],
            out_shape=[
                jax.ShapeDtypeStruct((tokens, d_model), F32),
                jax.ShapeDtypeStruct((batch, MLA_HEADS, QK_PAD, seq), BF16),
                jax.ShapeDtypeStruct((batch, MLA_HEADS, seq, QK_PAD), BF16),
                jax.ShapeDtypeStruct((batch, MLA_HEADS, MLA_V + ONES_ROWS, seq), BF16),
                jax.ShapeDtypeStruct((batch, DIFF_HEADS, dq_rows, 2 * seq), BF16),
                jax.ShapeDtypeStruct((batch, DIFF_HEADS, seq, DIFF_K_WIDTH), BF16),
                jax.ShapeDtypeStruct((batch, DIFF_HEADS, DIFF_V + ONES_ROWS, seq), BF16),
                jax.ShapeDtypeStruct((tokens // tm, 4 * DIFF_HEADS, LANES), F32),
            ],
            compiler_params=_params(1),
            name="pre",
        )(*[a for a, _ in pre_inputs])

        per_head = lambda *tail: pl.BlockSpec((1, 1) + tail, lambda b, hh: (b, hh, 0, 0))
        om = pl.pallas_call(
            _mla_attn_kernel,
            grid=(batch, MLA_HEADS),
            in_specs=[per_head(QK_PAD, seq), per_head(seq, QK_PAD), per_head(MLA_V + ONES_ROWS, seq)],
            out_specs=per_head(MLA_V, seq),
            out_shape=jax.ShapeDtypeStruct((batch, MLA_HEADS, MLA_V, seq), BF16),
            scratch_shapes=_attn_scratch(MLA_KV_CHUNK, MLA_Q_TILE, MLA_V + ONES_ROWS),
            compiler_params=_params(2),
            name="mla_attn",
        )(qm, km, vm)

        small = lambda shape: pl.BlockSpec(shape, lambda b, hh: (0,) * len(shape))
        od = pl.pallas_call(
            functools.partial(_diff_attn_kernel, lambda_init=lambda_init),
            grid=(batch, DIFF_HEADS),
            in_specs=[pl.BlockSpec(memory_space=pltpu.SMEM),
                      per_head(dq_rows, 2 * seq), per_head(seq, DIFF_K_WIDTH),
                      per_head(DIFF_V + ONES_ROWS, seq)]
                     + [small((1, DIFF_QK))] * 4 + [small((DIFF_V, 1))],
            out_specs=per_head(DIFF_V, seq),
            out_shape=jax.ShapeDtypeStruct((batch, DIFF_HEADS, DIFF_V, seq), BF16),
            scratch_shapes=_attn_scratch(DIFF_KV_CHUNK, 2 * DIFF_Q_TILE, DIFF_V + ONES_ROWS),
            compiler_params=_params(2),
            name="diff_attn",
        )(_alibi_reach(sq_norms, slopes2, batch, seq), qd, kd, vd, row(lambda_q1[i]), row(lambda_k1[i]), row(lambda_q2[i]),
          row(lambda_k2[i]), col(g_diff_sub[i]))

        mla_w = MLA_HEADS * MLA_V
        diff_w = DIFF_HEADS * DIFF_V
        tp = POST_TOKEN_TILE
        post_tiles_per_seq = seq // tp
        post_tok_spec = pl.BlockSpec((tp, d_model), lambda t: (t, 0))
        feat_t = lambda width: pl.BlockSpec(
            (1, width, tp), lambda t: (t // post_tiles_per_seq, 0, t % post_tiles_per_seq))
        post_inputs = [
            (h1, post_tok_spec),
            (om.reshape(batch, mla_w, seq), feat_t(mla_w)),
            (od.reshape(batch, diff_w, seq), feat_t(diff_w)),
            (p[i].reshape(tokens, ple_dim), pl.BlockSpec((tp, ple_dim), lambda t: (t, 0))),
            (bf(w_out[i, :mla_w]), None), (bf(w_out[i, mla_w:]), None),
            (row(g_ffn2[i]), None), (bf(w_ffn2_gate[i]), None), (bf(w_ffn2_up[i]), None),
            (bf(w_ffn2_down[i]), None),
            (row(g_ple_in[i]), None), (bf(w_ple_gate[i]), None), (row(b_ple_gate[i]), None),
            (bf(w_ple_proj[i]), None), (row(g_ple_out[i]), None),
        ]
        h = pl.pallas_call(
            _post_kernel,
            grid=(tokens // tp,),
            in_specs=[s if s is not None else _resident(a.shape) for a, s in post_inputs],
            out_specs=post_tok_spec,
            out_shape=jax.ShapeDtypeStruct((tokens, d_model), F32),
            compiler_params=_params(1),
            name="post",
        )(*[a for a, _ in post_inputs])

    return h.reshape(batch, seq, d_model)
```

```python
import functools
import math

import jax
import jax.numpy as jnp
import numpy as np
from jax import lax
from jax.experimental import pallas as pl
from jax.experimental.pallas import tpu as pltpu

F32 = jnp.float32
BF16 = jnp.bfloat16

MLA_HEADS = 8
MLA_NOPE = 64
MLA_ROPE = 32
MLA_QK = MLA_NOPE + MLA_ROPE
MLA_V = 64
MLA_Q_RANK = 256
MLA_KV_RANK = 128
ROPE_THETA = 10000.0
DIFF_HEADS = 4
DIFF_QK = 64
DIFF_V = 2 * DIFF_QK
EPS = 1e-6

QK_PAD = 128
ONES_ROWS = 16
AUG_ROWS = 16
DIFF_K_WIDTH = 256
POS_BLOCK = 256
NEG_BIG = -1e30
WEIGHT_CUTOFF_BITS = 130.0
LANES = 128
LOG2E = math.log2(math.e)

VMEM_LIMIT_BYTES = 56 * 1024 * 1024
TOKEN_TILE = 256
POST_TOKEN_TILE = 512
MLA_Q_TILE = 1024
DIFF_Q_TILE = 512
MLA_KV_CHUNK = 512
DIFF_KV_CHUNK = 512


def _dot(a, b):
    return jnp.dot(a, b, preferred_element_type=F32)


def _dot_nt(a, b):
    return lax.dot_general(a, b, (((1,), (1,)), ((), ())), preferred_element_type=F32)


def _dot_tn(a, b):
    return lax.dot_general(a, b, (((0,), (0,)), ((), ())), preferred_element_type=F32)


def _rms_rows(x, g_row):
    ms = jnp.mean(x * x, axis=-1, keepdims=True)
    return x * lax.rsqrt(ms + EPS) * g_row


def _rms_cols(xt, g_col):
    ms = jnp.mean(xt * xt, axis=0, keepdims=True)
    return xt * lax.rsqrt(ms + EPS) * g_col


def _swiglu(xn, wg_ref, wu_ref, wd_ref):
    gate = _dot(xn, wg_ref[...])
    up = _dot(xn, wu_ref[...])
    act = (gate / (1.0 + jnp.exp(-gate))) * up
    return _dot(act.astype(BF16), wd_ref[...])


def _rope_cols(xt, cos, sin):
    half = MLA_ROPE // 2
    rot = jnp.concatenate([-xt[half:], xt[:half]], axis=0)
    return xt * cos + rot * sin


def _pre_kernel(x_ref, g1_ref, wg_ref, wu_ref, wd_ref, gmix_ref, win_t_ref,
                gql_ref, wqu_t_ref, gkvl_ref, wkvu_t_ref,
                gqn_ref, gqr_ref, gkn_ref, gkr_ref, gdq_ref, gdk_ref, cos_ref, sin_ref,
                h1_ref, qm_ref, km_ref, vm_ref, qd_ref, kd_ref, vd_ref, norm_ref,
                *, slope_parts, tiles_per_seq):
    t = x_ref.shape[0]
    x = x_ref[...]
    h1 = x + 0.5 * _swiglu(_rms_rows(x, g1_ref[...]).astype(BF16), wg_ref, wu_ref, wd_ref)
    h1_ref[...] = h1

    hn = _rms_rows(h1, gmix_ref[...]).astype(BF16)
    ut = _dot_nt(win_t_ref[...], hn)
    o_kv = MLA_Q_RANK
    o_kr = o_kv + MLA_KV_RANK
    o_qd = o_kr + MLA_ROPE
    o_kd = o_qd + DIFF_HEADS * 2 * DIFF_QK
    o_vd = o_kd + DIFF_HEADS * 2 * DIFF_QK

    cos = cos_ref[...]
    sin = sin_ref[...]
    ones = jnp.ones((ONES_ROWS, t), BF16)
    zeros_pad = jnp.zeros((QK_PAD - MLA_QK, t), F32)

    q_scale = MLA_QK ** -0.5 * LOG2E
    qln = _rms_cols(ut[0:o_kv], gql_ref[...]).astype(BF16)
    qt = _dot(wqu_t_ref[...], qln)
    kvn = _rms_cols(ut[o_kv:o_kr], gkvl_ref[...]).astype(BF16)
    kvt = _dot(wkvu_t_ref[...], kvn)
    k_rot = _rope_cols(_rms_cols(ut[o_kr:o_qd], gkr_ref[...]), cos, sin)
    for h in range(MLA_HEADS):
        qh = qt[h * MLA_QK:(h + 1) * MLA_QK]
        q_nope = _rms_cols(qh[:MLA_NOPE], gqn_ref[...])
        q_rot = _rope_cols(_rms_cols(qh[MLA_NOPE:], gqr_ref[...]), cos, sin)
        qm_ref[0, h, 0:MLA_NOPE, :] = (q_nope * q_scale).astype(BF16)
        qm_ref[0, h, MLA_NOPE:MLA_QK, :] = (q_rot * q_scale).astype(BF16)
        qm_ref[0, h, MLA_QK:QK_PAD, :] = zeros_pad.astype(BF16)
        kvh = kvt[h * (MLA_NOPE + MLA_V):(h + 1) * (MLA_NOPE + MLA_V)]
        k_nope = _rms_cols(kvh[:MLA_NOPE], gkn_ref[...])
        k_full = jnp.concatenate([k_nope, k_rot, zeros_pad], axis=0)
        km_ref[0, h] = k_full.T.astype(BF16)
        vm_ref[0, h, 0:MLA_V, :] = kvh[MLA_NOPE:].astype(BF16)
        vm_ref[0, h, MLA_V:MLA_V + ONES_ROWS, :] = ones

    d_scale = DIFF_QK ** -0.5 * LOG2E
    zero_q = jnp.zeros((DIFF_QK, t), BF16)
    n_maps = 2 * DIFF_HEADS

    assert t == POS_BLOCK
    blk = (pl.program_id(0) % tiles_per_seq).astype(F32)

    def pick(index, values):
        out = jnp.zeros(index.shape, F32)
        for n, v in enumerate(values):
            out = jnp.where(index == n, v, out)
        return out

    def q_aug(hi, lo):
        r = lax.broadcasted_iota(jnp.int32, (AUG_ROWS, 2 * t), 0)
        rem = (lax.broadcasted_iota(jnp.int32, (AUG_ROWS, 2 * t), 1) % t).astype(F32)
        return pick(r, [blk, blk, rem, rem, POS_BLOCK * hi, POS_BLOCK * lo, hi, lo]).astype(BF16)

    def k_aug(hi, lo):
        width = DIFF_K_WIDTH - 2 * DIFF_QK
        c = lax.broadcasted_iota(jnp.int32, (t, width), 1)
        rem = lax.broadcasted_iota(jnp.int32, (t, width), 0).astype(F32)
        return pick(c, [-POS_BLOCK * hi, -POS_BLOCK * lo, -hi, -lo, blk, blk, rem, rem]).astype(BF16)

    def put_max_sq_norm(row, xt):
        sq = jnp.max(jnp.sum(xt * xt, axis=0, keepdims=True), axis=1, keepdims=True)
        norm_ref[0, row:row + 1, :] = jnp.broadcast_to(sq, (1, norm_ref.shape[2]))

    for h in range(DIFF_HEADS):
        k_maps = []
        for j in range(2):
            r = (2 * h + j) * DIFF_QK
            qj = _rms_cols(ut[o_qd + r:o_qd + r + DIFF_QK], gdq_ref[...]) * d_scale
            kj = _rms_cols(ut[o_kd + r:o_kd + r + DIFF_QK], gdk_ref[...])
            put_max_sq_norm(2 * h + j, qj)
            put_max_sq_norm(n_maps + 2 * h + j, kj)
            qd_ref[0, h, j * DIFF_QK:(j + 1) * DIFF_QK, j * t:(j + 1) * t] = qj.astype(BF16)
            qd_ref[0, h, (1 - j) * DIFF_QK:(2 - j) * DIFF_QK, j * t:(j + 1) * t] = zero_q
            k_maps.append(kj)
        qd_ref[0, h, 2 * DIFF_QK:2 * DIFF_QK + AUG_ROWS, :] = q_aug(*slope_parts[h][1:])
        k12 = jnp.concatenate(k_maps, axis=0)
        kd_ref[0, h, :, 0:2 * DIFF_QK] = k12.T.astype(BF16)
        kd_ref[0, h, :, 2 * DIFF_QK:DIFF_K_WIDTH] = k_aug(*slope_parts[h][1:])
        vd_ref[0, h, 0:DIFF_V, :] = ut[o_vd + h * DIFF_V:o_vd + (h + 1) * DIFF_V].astype(BF16)
        vd_ref[0, h, DIFF_V:DIFF_V + ONES_ROWS, :] = ones


def _key_rows(c, size):
    return pl.ds(pl.multiple_of(c * size, size), size)


def _sweep_head(*, n_tiles, width, window, first_chunk, scores, v_chunk, finalize, p_buf, acc_buf):
    def chunk_of(qi, i, k):
        lo, count = window(qi)
        return lo + (first_chunk(qi) - lo + 2 * i + k) % count

    def local_softmax(st, slot):
        sb = st.astype(BF16)
        chunk_max = jnp.max(sb, axis=0, keepdims=True)
        p_buf[slot] = jnp.exp2(sb - chunk_max)
        return chunk_max.astype(F32)

    def step(qi, i, parity, carry, last=False):
        m, chunk_maxes = carry
        nxt_qi, nxt_i = (qi + 1, 0) if last else (qi, i + 1)
        nxt_qi = jnp.minimum(nxt_qi, n_tiles - 1)
        next_maxes = tuple(
            local_softmax(scores(nxt_qi, chunk_of(nxt_qi, nxt_i, k), last and k == 0), (1 - parity, k))
            for k in range(2))

        m = jnp.where(i == 0, NEG_BIG, m)
        m_new = jnp.maximum(m, jnp.maximum(*chunk_maxes))
        acc = acc_buf[...] * jnp.exp2(m - m_new)
        for k in range(2):
            acc = acc + (_dot(v_chunk(chunk_of(qi, i, k)), p_buf[parity, k])
                         * jnp.exp2(chunk_maxes[k] - m_new))
        acc_buf[...] = acc
        return m_new, next_maxes

    def tile(qi, carry):
        def two_steps(j, cr):
            cr = step(qi, 2 * j, 0, cr)
            return step(qi, 2 * j + 1, 1, cr)
        n_pairs = window(qi)[1] // 2
        carry = lax.fori_loop(0, n_pairs // 2 - 1, two_steps, carry)
        carry = step(qi, n_pairs - 2, 0, carry)
        carry = step(qi, n_pairs - 1, 1, carry, last=True)
        finalize(qi)
        return carry

    acc_buf[...] = jnp.zeros(acc_buf.shape, acc_buf.dtype)
    first_maxes = tuple(local_softmax(scores(0, chunk_of(0, 0, k), k == 0), (0, k)) for k in range(2))
    lax.fori_loop(0, n_tiles, tile, (jnp.zeros((1, width), F32), first_maxes))


def _mla_attn_kernel(q_ref, k_ref, v_ref, o_ref, p_buf, acc_buf):
    chunk, tq = p_buf.shape[2:]
    seq = k_ref.shape[2]

    def tile_cols(qi):
        return pl.ds(pl.multiple_of(qi * tq, tq), tq)

    def scores(qi, c, is_first_chunk):
        return _dot(k_ref[0, 0, _key_rows(c, chunk), :], q_ref[0, 0, :, tile_cols(qi)])

    def finalize(qi):
        o_ref[0, 0, :, tile_cols(qi)] = (
            acc_buf[0:MLA_V, :] / acc_buf[MLA_V:MLA_V + 1, :]).astype(BF16)

    _sweep_head(n_tiles=seq // tq, width=tq, window=lambda qi: (0, seq // chunk),
                first_chunk=lambda qi: 0, scores=scores,
                v_chunk=lambda c: v_ref[0, 0, :, _key_rows(c, chunk)], finalize=finalize,
                p_buf=p_buf, acc_buf=acc_buf)


def _diff_attn_kernel(reach_ref, q_ref, k_ref, v_ref, lq1_ref, lk1_ref, lq2_ref, lk2_ref, gsub_ref,
                      o_ref, p_buf, acc_buf, *, lambda_init):
    chunk, tq = p_buf.shape[2], p_buf.shape[3] // 2
    seq = k_ref.shape[2]
    n_chunks = seq // chunk
    assert chunk % tq == 0 and n_chunks % 4 == 0
    first_chunk = lambda qi: (qi * tq) // chunk
    reach = reach_ref[pl.program_id(0) * pl.num_programs(1) + pl.program_id(1)]

    def window(qi):
        lo = jnp.maximum(qi * tq - reach, 0) // chunk
        hi = jnp.minimum(qi * tq + tq - 1 + reach, seq - 1) // chunk
        count = jnp.minimum((hi - lo + 4) // 4 * 4, n_chunks)
        return jnp.minimum(lo, n_chunks - count), count

    lam = (jnp.exp(jnp.sum(lq1_ref[...] * lk1_ref[...], axis=-1, keepdims=True))
           - jnp.exp(jnp.sum(lq2_ref[...] * lk2_ref[...], axis=-1, keepdims=True)) + lambda_init)
    zero_rows = jnp.zeros((DIFF_K_WIDTH - 2 * DIFF_QK - AUG_ROWS, 2 * tq), BF16)

    def scores(qi, c, is_first_chunk):
        qa = q_ref[0, 0, :, pl.ds(pl.multiple_of(qi * 2 * tq, 2 * tq), 2 * tq)]
        kc = k_ref[0, 0, _key_rows(c, chunk), :]

        def signed(sign):
            aug = (qa[2 * DIFF_QK:].astype(F32) * sign).astype(BF16)
            return _dot(kc, jnp.concatenate([qa[:2 * DIFF_QK], aug, zero_rows], axis=0))

        if is_first_chunk:
            return jnp.minimum(signed(1.0), signed(-1.0))
        return signed(jnp.where(c < first_chunk(qi), 1.0, -1.0).astype(F32))

    def finalize(qi):
        o_all = acc_buf[0:DIFF_V, :] / acc_buf[DIFF_V:DIFF_V + 1, :]
        sub = POS_BLOCK
        o = jnp.concatenate(
            [o_all[:, 2 * j * sub:(2 * j + 1) * sub] - lam * o_all[:, (2 * j + 1) * sub:(2 * j + 2) * sub]
             for j in range(tq // sub)], axis=1)
        o_ref[0, 0, :, pl.ds(pl.multiple_of(qi * tq, tq), tq)] = (
            _rms_cols(o, gsub_ref[...]) * (1.0 - lambda_init)).astype(BF16)

    _sweep_head(n_tiles=seq // tq, width=2 * tq, window=window,
                first_chunk=first_chunk, scores=scores,
                v_chunk=lambda c: v_ref[0, 0, :, _key_rows(c, chunk)], finalize=finalize,
                p_buf=p_buf, acc_buf=acc_buf)


def _post_kernel(h1_ref, om_ref, od_ref, p_ref, wom_ref, wod_ref, g2_ref, wg_ref, wu_ref, wd_ref,
                 gpi_ref, wpg_ref, bpg_ref, wpp_ref, gpo_ref, out_ref):
    h = h1_ref[...] + _dot_tn(om_ref[0], wom_ref[...]) + _dot_tn(od_ref[0], wod_ref[...])
    h = h + 0.5 * _swiglu(_rms_rows(h, g2_ref[...]).astype(BF16), wg_ref, wu_ref, wd_ref)
    z = _dot(_rms_rows(h, gpi_ref[...]).astype(BF16), wpg_ref[...]) + bpg_ref[...]
    gate = 1.0 / (1.0 + jnp.exp(-z))
    emb = _rms_rows(_dot(p_ref[...].astype(BF16), wpp_ref[...]), gpo_ref[...])
    out_ref[...] = h + gate * emb


def _resident(shape):
    nd = len(shape)
    return pl.BlockSpec(shape, lambda *_: (0,) * nd, pipeline_mode=pl.Buffered(1))


def _params(n_axes):
    return pltpu.CompilerParams(dimension_semantics=("arbitrary",) * n_axes,
                                vmem_limit_bytes=VMEM_LIMIT_BYTES)


def _attn_scratch(chunk, width, rows):
    return [pltpu.VMEM((2, 2, chunk, width), BF16), pltpu.VMEM((rows, width), F32)]


def _rope_tables_t(seq):
    pos = jnp.arange(seq, dtype=F32)
    inv = ROPE_THETA ** (-jnp.arange(0, MLA_ROPE, 2, dtype=F32) / MLA_ROPE)
    ang = pos[:, None] * inv[None, :]
    ang = jnp.concatenate([ang, ang], axis=-1)
    return jnp.cos(ang).T, jnp.sin(ang).T


def _alibi_reach(sq_norms, slopes2, batch, seq):
    n_maps = 2 * DIFF_HEADS
    top = jnp.max(sq_norms[:, :, 0].reshape(batch, -1, 2 * n_maps), axis=1)
    bound = jnp.sqrt(jnp.max((top[:, :n_maps] * top[:, n_maps:]).reshape(batch, DIFF_HEADS, 2), axis=-1))
    bound = bound * 1.02
    reach = jnp.ceil((2.0 * bound + WEIGHT_CUTOFF_BITS) / slopes2[None, :])
    return jnp.minimum(reach, float(seq)).astype(jnp.int32).reshape(-1)


def _alibi_slope_parts():
    parts = []
    for h in range(DIFF_HEADS):
        s2 = np.float32(LOG2E * 2.0 ** (-8.0 * (h + 1) / DIFF_HEADS))
        bits = s2.view(np.uint32)
        hi = np.uint32((bits + np.uint32(0x7FFF) + ((bits >> np.uint32(16)) & np.uint32(1)))
                       & np.uint32(0xFFFF0000)).view(np.float32)
        parts.append((float(s2), float(hi), float(s2 - hi)))
    return tuple(parts)


def kernel(x, p, g_ffn1, w_ffn1_gate, w_ffn1_up, w_ffn1_down, g_mix, w_in, g_q_lat, w_q_up, g_kv_lat, w_kv_up, g_mla_q, g_mla_k, g_diff_q, g_diff_k, lambda_q1, lambda_k1, lambda_q2, lambda_k2, g_diff_sub, w_out, g_ffn2, w_ffn2_gate, w_ffn2_up, w_ffn2_down, g_ple_in, w_ple_gate, b_ple_gate, w_ple_proj, g_ple_out):
    batch, seq, d_model = x.shape
    depth = p.shape[0]
    tokens = batch * seq
    tm = TOKEN_TILE
    tiles_per_seq = seq // tm
    assert seq % tm == 0 and seq % POST_TOKEN_TILE == 0 and seq % MLA_Q_TILE == 0
    assert seq % (4 * MLA_KV_CHUNK) == 0 and seq % (4 * DIFF_KV_CHUNK) == 0
    assert tm == POS_BLOCK and DIFF_Q_TILE % POS_BLOCK == 0 and seq % DIFF_Q_TILE == 0

    cos_t, sin_t = _rope_tables_t(seq)
    slope_parts = _alibi_slope_parts()
    slopes2 = jnp.asarray([s2 for s2, _, _ in slope_parts], F32)
    row = lambda v: v.reshape(1, -1)
    col = lambda v: v.reshape(-1, 1)
    bf = lambda w: w.astype(BF16)

    h = x.reshape(tokens, d_model)
    for i in range(depth):
        lambda_init = 0.8 - 0.6 * math.exp(-0.3 * i)
        ple_dim = p.shape[-1]

        tok_spec = pl.BlockSpec((tm, d_model), lambda t: (t, 0))
        head_t = lambda heads, rows, width: pl.BlockSpec(
            (1, heads, rows, width), lambda t: (t // tiles_per_seq, 0, 0, t % tiles_per_seq))
        head_r = lambda heads, width: pl.BlockSpec(
            (1, heads, tm, width), lambda t: (t // tiles_per_seq, 0, t % tiles_per_seq, 0))
        rope_spec = pl.BlockSpec((MLA_ROPE, tm), lambda t: (0, t % tiles_per_seq))
        pre_inputs = [
            (h, tok_spec),
            (row(g_ffn1[i]), None), (bf(w_ffn1_gate[i]), None), (bf(w_ffn1_up[i]), None),
            (bf(w_ffn1_down[i]), None), (row(g_mix[i]), None), (bf(w_in[i].T), None),
            (col(g_q_lat[i]), None), (bf(w_q_up[i].T), None),
            (col(g_kv_lat[i]), None), (bf(w_kv_up[i].T), None),
            (col(g_mla_q[i, :MLA_NOPE]), None), (col(g_mla_q[i, MLA_NOPE:]), None),
            (col(g_mla_k[i, :MLA_NOPE]), None), (col(g_mla_k[i, MLA_NOPE:]), None),
            (col(g_diff_q[i]), None), (col(g_diff_k[i]), None),
            (cos_t, rope_spec), (sin_t, rope_spec),
        ]
        dq_rows = 2 * DIFF_QK + AUG_ROWS
        h1, qm, km, vm, qd, kd, vd, sq_norms = pl.pallas_call(
            functools.partial(_pre_kernel, slope_parts=slope_parts, tiles_per_seq=tiles_per_seq),
            grid=(tokens // tm,),
            in_specs=[s if s is not None else _resident(a.shape) for a, s in pre_inputs],
            out_specs=[tok_spec, head_t(MLA_HEADS, QK_PAD, tm), head_r(MLA_HEADS, QK_PAD),
                       head_t(MLA_HEADS, MLA_V + ONES_ROWS, tm), head_t(DIFF_HEADS, dq_rows, 2 * tm),
                       head_r(DIFF_HEADS, DIFF_K_WIDTH), head_t(DIFF_HEADS, DIFF_V + ONES_ROWS, tm),
                       pl.BlockSpec((1, 4 * DIFF_HEADS, LANES), lambda t: (t, 0, 0))],
            out_shape=[
                jax.ShapeDtypeStruct((tokens, d_model), F32),
                jax.ShapeDtypeStruct((batch, MLA_HEADS, QK_PAD, seq), BF16),
                jax.ShapeDtypeStruct((batch, MLA_HEADS, seq, QK_PAD), BF16),
                jax.ShapeDtypeStruct((batch, MLA_HEADS, MLA_V + ONES_ROWS, seq), BF16),
                jax.ShapeDtypeStruct((batch, DIFF_HEADS, dq_rows, 2 * seq), BF16),
                jax.ShapeDtypeStruct((batch, DIFF_HEADS, seq, DIFF_K_WIDTH), BF16),
                jax.ShapeDtypeStruct((batch, DIFF_HEADS, DIFF_V + ONES_ROWS, seq), BF16),
                jax.ShapeDtypeStruct((tokens // tm, 4 * DIFF_HEADS, LANES), F32),
            ],
            compiler_params=_params(1),
            name="pre",
        )(*[a for a, _ in pre_inputs])

        per_head = lambda *tail: pl.BlockSpec((1, 1) + tail, lambda b, hh: (b, hh, 0, 0))
        om = pl.pallas_call(
            _mla_attn_kernel,
            grid=(batch, MLA_HEADS),
            in_specs=[per_head(QK_PAD, seq), per_head(seq, QK_PAD), per_head(MLA_V + ONES_ROWS, seq)],
            out_specs=per_head(MLA_V, seq),
            out_shape=jax.ShapeDtypeStruct((batch, MLA_HEADS, MLA_V, seq), BF16),
            scratch_shapes=_attn_scratch(MLA_KV_CHUNK, MLA_Q_TILE, MLA_V + ONES_ROWS),
            compiler_params=_params(2),
            name="mla_attn",
        )(qm, km, vm)

        small = lambda shape: pl.BlockSpec(shape, lambda b, hh: (0,) * len(shape))
        od = pl.pallas_call(
            functools.partial(_diff_attn_kernel, lambda_init=lambda_init),
            grid=(batch, DIFF_HEADS),
            in_specs=[pl.BlockSpec(memory_space=pltpu.SMEM),
                      per_head(dq_rows, 2 * seq), per_head(seq, DIFF_K_WIDTH),
                      per_head(DIFF_V + ONES_ROWS, seq)]
                     + [small((1, DIFF_QK))] * 4 + [small((DIFF_V, 1))],
            out_specs=per_head(DIFF_V, seq),
            out_shape=jax.ShapeDtypeStruct((batch, DIFF_HEADS, DIFF_V, seq), BF16),
            scratch_shapes=_attn_scratch(DIFF_KV_CHUNK, 2 * DIFF_Q_TILE, DIFF_V + ONES_ROWS),
            compiler_params=_params(2),
            name="diff_attn",
        )(_alibi_reach(sq_norms, slopes2, batch, seq), qd, kd, vd, row(lambda_q1[i]), row(lambda_k1[i]), row(lambda_q2[i]),
          row(lambda_k2[i]), col(g_diff_sub[i]))

        mla_w = MLA_HEADS * MLA_V
        diff_w = DIFF_HEADS * DIFF_V
        tp = POST_TOKEN_TILE
        post_tiles_per_seq = seq // tp
        post_tok_spec = pl.BlockSpec((tp, d_model), lambda t: (t, 0))
        feat_t = lambda width: pl.BlockSpec(
            (1, width, tp), lambda t: (t // post_tiles_per_seq, 0, t % post_tiles_per_seq))
        post_inputs = [
            (h1, post_tok_spec),
            (om.reshape(batch, mla_w, seq), feat_t(mla_w)),
            (od.reshape(batch, diff_w, seq), feat_t(diff_w)),
            (p[i].reshape(tokens, ple_dim), pl.BlockSpec((tp, ple_dim), lambda t: (t, 0))),
            (bf(w_out[i, :mla_w]), None), (bf(w_out[i, mla_w:]), None),
            (row(g_ffn2[i]), None), (bf(w_ffn2_gate[i]), None), (bf(w_ffn2_up[i]), None),
            (bf(w_ffn2_down[i]), None),
            (row(g_ple_in[i]), None), (bf(w_ple_gate[i]), None), (row(b_ple_gate[i]), None),
            (bf(w_ple_proj[i]), None), (row(g_ple_out[i]), None),
        ]
        h = pl.pallas_call(
            _post_kernel,
            grid=(tokens // tp,),
            in_specs=[s if s is not None else _resident(a.shape) for a, s in post_inputs],
            out_specs=post_tok_spec,
            out_shape=jax.ShapeDtypeStruct((tokens, d_model), F32),
            compiler_params=_params(1),
            name="post",
        )(*[a for a, _ in post_inputs])

    return h.reshape(batch, seq, d_model)
```

```python
import functools
import math

import jax
import jax.numpy as jnp
import numpy as np
from jax import lax
from jax.experimental import pallas as pl
from jax.experimental.pallas import tpu as pltpu

F32 = jnp.float32
BF16 = jnp.bfloat16

MLA_HEADS = 8
MLA_NOPE = 64
MLA_ROPE = 32
MLA_QK = MLA_NOPE + MLA_ROPE
MLA_V = 64
MLA_Q_RANK = 256
MLA_KV_RANK = 128
ROPE_THETA = 10000.0
DIFF_HEADS = 4
DIFF_QK = 64
DIFF_V = 2 * DIFF_QK
EPS = 1e-6

QK_PAD = 128
ONES_ROWS = 16
AUG_ROWS = 16
DIFF_K_WIDTH = 256
POS_BLOCK = 256
NEG_BIG = -1e30
WEIGHT_CUTOFF_BITS = 130.0
LANES = 128
LOG2E = math.log2(math.e)

VMEM_LIMIT_BYTES = 56 * 1024 * 1024
TOKEN_TILE = 256
POST_TOKEN_TILE = 512
FFN_STAGE_BLOCKS = 8
MLA_Q_TILE = 1024
DIFF_Q_TILE = 512
MLA_KV_CHUNK = 512
DIFF_KV_CHUNK = 512


def _dot(a, b):
    return jnp.dot(a, b, preferred_element_type=F32)


def _dot_nt(a, b):
    return lax.dot_general(a, b, (((1,), (1,)), ((), ())), preferred_element_type=F32)


def _dot_tn(a, b):
    return lax.dot_general(a, b, (((0,), (0,)), ((), ())), preferred_element_type=F32)


def _rms_rows(x, g_row):
    ms = jnp.mean(x * x, axis=-1, keepdims=True)
    return x * lax.rsqrt(ms + EPS) * g_row


def _rms_cols(xt, g_col):
    ms = jnp.mean(xt * xt, axis=0, keepdims=True)
    return xt * lax.rsqrt(ms + EPS) * g_col


def _swiglu(xn, wg_ref, wu_ref, wd_ref):
    gate = _dot(xn, wg_ref[...])
    up = _dot(xn, wu_ref[...])
    act = (gate / (1.0 + jnp.exp(-gate))) * up
    return _dot(act.astype(BF16), wd_ref[...])


def _load_as_bf16(src_hbm, dst, stage, sem):
    rows = stage.shape[1]
    n_blocks = src_hbm.shape[0] // rows
    assert n_blocks * rows == src_hbm.shape[0] and stage.shape[2] == src_hbm.shape[1]

    def block_copy(c):
        return pltpu.make_async_copy(src_hbm.at[pl.ds(c * rows, rows), :], stage.at[c % 2], sem.at[c % 2])

    block_copy(0).start()
    for c in range(n_blocks):
        if c + 1 < n_blocks:
            block_copy(c + 1).start()
        block_copy(c).wait()
        dst[pl.ds(c * rows, rows), :] = stage[c % 2].astype(BF16)


def _load_ffn_weights(wg_hbm, wu_hbm, wd_hbm, wg_ref, wu_ref, wd_ref, stage_in, stage_out, sem):
    @pl.when(pl.program_id(0) == 0)
    def _():
        _load_as_bf16(wg_hbm, wg_ref, stage_in, sem)
        _load_as_bf16(wu_hbm, wu_ref, stage_in, sem)
        _load_as_bf16(wd_hbm, wd_ref, stage_out, sem)


def _rope_cols(xt, cos, sin):
    half = MLA_ROPE // 2
    rot = jnp.concatenate([-xt[half:], xt[:half]], axis=0)
    return xt * cos + rot * sin


def _pre_kernel(x_ref, g1_ref, wg_hbm, wu_hbm, wd_hbm, gmix_ref, win_t_ref,
                gql_ref, wqu_t_ref, gkvl_ref, wkvu_t_ref,
                gqn_ref, gqr_ref, gkn_ref, gkr_ref, gdq_ref, gdk_ref, cos_ref, sin_ref,
                h1_ref, qm_ref, km_ref, vm_ref, qd_ref, kd_ref, vd_ref, norm_ref,
                wg_ref, wu_ref, wd_ref, stage_in, stage_out, sem,
                *, slope_parts, tiles_per_seq):
    _load_ffn_weights(wg_hbm, wu_hbm, wd_hbm, wg_ref, wu_ref, wd_ref, stage_in, stage_out, sem)
    t = x_ref.shape[0]
    x = x_ref[...]
    h1 = x + 0.5 * _swiglu(_rms_rows(x, g1_ref[...]).astype(BF16), wg_ref, wu_ref, wd_ref)
    h1_ref[...] = h1

    hn = _rms_rows(h1, gmix_ref[...]).astype(BF16)
    ut = _dot_nt(win_t_ref[...], hn)
    o_kv = MLA_Q_RANK
    o_kr = o_kv + MLA_KV_RANK
    o_qd = o_kr + MLA_ROPE
    o_kd = o_qd + DIFF_HEADS * 2 * DIFF_QK
    o_vd = o_kd + DIFF_HEADS * 2 * DIFF_QK

    cos = cos_ref[...]
    sin = sin_ref[...]
    ones = jnp.ones((ONES_ROWS, t), BF16)
    zeros_pad = jnp.zeros((QK_PAD - MLA_QK, t), F32)

    q_scale = MLA_QK ** -0.5 * LOG2E
    qln = _rms_cols(ut[0:o_kv], gql_ref[...]).astype(BF16)
    qt = _dot(wqu_t_ref[...], qln)
    kvn = _rms_cols(ut[o_kv:o_kr], gkvl_ref[...]).astype(BF16)
    kvt = _dot(wkvu_t_ref[...], kvn)
    k_rot = _rope_cols(_rms_cols(ut[o_kr:o_qd], gkr_ref[...]), cos, sin)
    for h in range(MLA_HEADS):
        qh = qt[h * MLA_QK:(h + 1) * MLA_QK]
        q_nope = _rms_cols(qh[:MLA_NOPE], gqn_ref[...])
        q_rot = _rope_cols(_rms_cols(qh[MLA_NOPE:], gqr_ref[...]), cos, sin)
        qm_ref[0, h, 0:MLA_NOPE, :] = (q_nope * q_scale).astype(BF16)
        qm_ref[0, h, MLA_NOPE:MLA_QK, :] = (q_rot * q_scale).astype(BF16)
        qm_ref[0, h, MLA_QK:QK_PAD, :] = zeros_pad.astype(BF16)
        kvh = kvt[h * (MLA_NOPE + MLA_V):(h + 1) * (MLA_NOPE + MLA_V)]
        k_nope = _rms_cols(kvh[:MLA_NOPE], gkn_ref[...])
        k_full = jnp.concatenate([k_nope, k_rot, zeros_pad], axis=0)
        km_ref[0, h] = k_full.T.astype(BF16)
        vm_ref[0, h, 0:MLA_V, :] = kvh[MLA_NOPE:].astype(BF16)
        vm_ref[0, h, MLA_V:MLA_V + ONES_ROWS, :] = ones

    d_scale = DIFF_QK ** -0.5 * LOG2E
    zero_q = jnp.zeros((DIFF_QK, t), BF16)
    n_maps = 2 * DIFF_HEADS

    assert t == POS_BLOCK
    blk = (pl.program_id(0) % tiles_per_seq).astype(F32)

    def pick(index, values):
        out = jnp.zeros(index.shape, F32)
        for n, v in enumerate(values):
            out = jnp.where(index == n, v, out)
        return out

    def q_aug(hi, lo):
        r = lax.broadcasted_iota(jnp.int32, (AUG_ROWS, 2 * t), 0)
        rem = (lax.broadcasted_iota(jnp.int32, (AUG_ROWS, 2 * t), 1) % t).astype(F32)
        return pick(r, [blk, blk, rem, rem, POS_BLOCK * hi, POS_BLOCK * lo, hi, lo]).astype(BF16)

    def k_aug(hi, lo):
        width = DIFF_K_WIDTH - 2 * DIFF_QK
        c = lax.broadcasted_iota(jnp.int32, (t, width), 1)
        rem = lax.broadcasted_iota(jnp.int32, (t, width), 0).astype(F32)
        return pick(c, [-POS_BLOCK * hi, -POS_BLOCK * lo, -hi, -lo, blk, blk, rem, rem]).astype(BF16)

    def put_max_sq_norm(row, xt):
        sq = jnp.max(jnp.sum(xt * xt, axis=0, keepdims=True), axis=1, keepdims=True)
        norm_ref[0, row:row + 1, :] = jnp.broadcast_to(sq, (1, norm_ref.shape[2]))

    for h in range(DIFF_HEADS):
        k_maps = []
        for j in range(2):
            r = (2 * h + j) * DIFF_QK
            qj = _rms_cols(ut[o_qd + r:o_qd + r + DIFF_QK], gdq_ref[...]) * d_scale
            kj = _rms_cols(ut[o_kd + r:o_kd + r + DIFF_QK], gdk_ref[...])
            put_max_sq_norm(2 * h + j, qj)
            put_max_sq_norm(n_maps + 2 * h + j, kj)
            qd_ref[0, h, j * DIFF_QK:(j + 1) * DIFF_QK, j * t:(j + 1) * t] = qj.astype(BF16)
            qd_ref[0, h, (1 - j) * DIFF_QK:(2 - j) * DIFF_QK, j * t:(j + 1) * t] = zero_q
            k_maps.append(kj)
        qd_ref[0, h, 2 * DIFF_QK:2 * DIFF_QK + AUG_ROWS, :] = q_aug(*slope_parts[h][1:])
        k12 = jnp.concatenate(k_maps, axis=0)
        kd_ref[0, h, :, 0:2 * DIFF_QK] = k12.T.astype(BF16)
        kd_ref[0, h, :, 2 * DIFF_QK:DIFF_K_WIDTH] = k_aug(*slope_parts[h][1:])
        vd_ref[0, h, 0:DIFF_V, :] = ut[o_vd + h * DIFF_V:o_vd + (h + 1) * DIFF_V].astype(BF16)
        vd_ref[0, h, DIFF_V:DIFF_V + ONES_ROWS, :] = ones


def _key_rows(c, size):
    return pl.ds(pl.multiple_of(c * size, size), size)


def _store_scores(s_buf, max_buf, slot, scores):
    sb = scores.astype(BF16)
    s_buf[slot] = sb
    max_buf[slot] = jnp.max(sb, axis=0, keepdims=True).astype(F32)


def _sweep_head(*, n_tiles, width, window, first_chunk, issue_scores, v_chunk, finalize,
                s_buf, max_buf, p_buf, acc_buf):
    def chunk_of(qi, i, k):
        lo, count = window(qi)
        return lo + (first_chunk(qi) - lo + 2 * i + k) % count

    def step(qi, i, parity, carry, first=False, last=False):
        m, alphas = carry
        cur = [(s_buf[parity, k], max_buf[parity, k]) for k in range(2)]

        nxt_qi, nxt_i = (qi + 1, 0) if last else (qi, i + 1)
        nxt_qi = jnp.minimum(nxt_qi, n_tiles - 1)
        for k in range(2):
            issue_scores(nxt_qi, chunk_of(nxt_qi, nxt_i, k), last and k == 0, (1 - parity, k))

        if first:
            prv_qi = jnp.maximum(qi - 1, 0)
            prv_i = window(prv_qi)[1] // 2 - 1
        else:
            prv_qi, prv_i = qi, i - 1
        for k in range(2):
            acc_buf[...] = (acc_buf[...] * alphas[k]
                            + _dot(v_chunk(chunk_of(prv_qi, prv_i, k)), p_buf[1 - parity, k]))

        if first:
            m = jnp.full((1, width), NEG_BIG, F32)
        alphas = []
        for k, (sb, chunk_max) in enumerate(cur):
            m_new = jnp.maximum(m, chunk_max)
            p_buf[parity, k] = jnp.exp2(sb - m_new.astype(BF16))
            alphas.append(jnp.exp2(m - m_new))
            m = m_new
        return m, tuple(alphas)

    def tile(qi, carry):
        def two_steps(j, cr):
            cr = step(qi, 2 * j + 1, 1, cr)
            return step(qi, 2 * j + 2, 0, cr)
        n_pairs = window(qi)[1] // 2
        carry = lax.fori_loop(0, n_pairs // 2 - 1, two_steps, carry)
        carry = step(qi, n_pairs - 1, 1, carry, last=True)
        carry = step(qi + 1, 0, 0, carry, first=True)
        finalize(qi)
        return carry

    p_buf[...] = jnp.zeros(p_buf.shape, p_buf.dtype)
    acc_buf[...] = jnp.zeros(acc_buf.shape, acc_buf.dtype)
    for k in range(2):
        issue_scores(0, chunk_of(0, 0, k), k == 0, (0, k))
    zero_row = jnp.zeros((1, width), F32)
    carry = (zero_row, (zero_row, zero_row))
    carry = step(0, 0, 0, carry, first=True)
    lax.fori_loop(0, n_tiles, tile, carry)


def _mla_attn_kernel(q_ref, k_ref, v_ref, o_ref, s_buf, max_buf, p_buf, acc_buf):
    chunk, tq = s_buf.shape[2:]
    seq = k_ref.shape[2]

    def tile_cols(qi):
        return pl.ds(pl.multiple_of(qi * tq, tq), tq)

    def issue_scores(qi, c, is_first_chunk, slot):
        _store_scores(s_buf, max_buf, slot,
                      _dot(k_ref[0, 0, _key_rows(c, chunk), :], q_ref[0, 0, :, tile_cols(qi)]))

    def finalize(qi):
        o_ref[0, 0, :, tile_cols(qi)] = (
            acc_buf[0:MLA_V, :] / acc_buf[MLA_V:MLA_V + 1, :]).astype(BF16)

    _sweep_head(n_tiles=seq // tq, width=tq, window=lambda qi: (0, seq // chunk),
                first_chunk=lambda qi: 0, issue_scores=issue_scores,
                v_chunk=lambda c: v_ref[0, 0, :, _key_rows(c, chunk)], finalize=finalize,
                s_buf=s_buf, max_buf=max_buf, p_buf=p_buf, acc_buf=acc_buf)


def _diff_attn_kernel(reach_ref, q_ref, k_ref, v_ref, lq1_ref, lk1_ref, lq2_ref, lk2_ref, gsub_ref,
                      o_ref, s_buf, max_buf, p_buf, acc_buf, *, lambda_init):
    chunk, tq = s_buf.shape[2], s_buf.shape[3] // 2
    seq = k_ref.shape[2]
    n_chunks = seq // chunk
    assert chunk % tq == 0 and n_chunks % 4 == 0
    first_chunk = lambda qi: (qi * tq) // chunk
    reach = reach_ref[pl.program_id(0) * pl.num_programs(1) + pl.program_id(1)]

    def window(qi):
        lo = jnp.maximum(qi * tq - reach, 0) // chunk
        hi = jnp.minimum(qi * tq + tq - 1 + reach, seq - 1) // chunk
        count = jnp.minimum((hi - lo + 4) // 4 * 4, n_chunks)
        return jnp.minimum(lo, n_chunks - count), count

    lam = (jnp.exp(jnp.sum(lq1_ref[...] * lk1_ref[...], axis=-1, keepdims=True))
           - jnp.exp(jnp.sum(lq2_ref[...] * lk2_ref[...], axis=-1, keepdims=True)) + lambda_init)
    zero_rows = jnp.zeros((DIFF_K_WIDTH - 2 * DIFF_QK - AUG_ROWS, 2 * tq), BF16)

    def issue_scores(qi, c, is_first_chunk, slot):
        qa = q_ref[0, 0, :, pl.ds(pl.multiple_of(qi * 2 * tq, 2 * tq), 2 * tq)]
        kc = k_ref[0, 0, _key_rows(c, chunk), :]

        def scores(sign):
            aug = (qa[2 * DIFF_QK:].astype(F32) * sign).astype(BF16)
            return _dot(kc, jnp.concatenate([qa[:2 * DIFF_QK], aug, zero_rows], axis=0))

        if is_first_chunk:
            st = jnp.minimum(scores(1.0), scores(-1.0))
        else:
            st = scores(jnp.where(c < first_chunk(qi), 1.0, -1.0).astype(F32))
        _store_scores(s_buf, max_buf, slot, st)

    def finalize(qi):
        o_all = acc_buf[0:DIFF_V, :] / acc_buf[DIFF_V:DIFF_V + 1, :]
        sub = POS_BLOCK
        o = jnp.concatenate(
            [o_all[:, 2 * j * sub:(2 * j + 1) * sub] - lam * o_all[:, (2 * j + 1) * sub:(2 * j + 2) * sub]
             for j in range(tq // sub)], axis=1)
        o_ref[0, 0, :, pl.ds(pl.multiple_of(qi * tq, tq), tq)] = (
            _rms_cols(o, gsub_ref[...]) * (1.0 - lambda_init)).astype(BF16)

    _sweep_head(n_tiles=seq // tq, width=2 * tq, window=window,
                first_chunk=first_chunk, issue_scores=issue_scores,
                v_chunk=lambda c: v_ref[0, 0, :, _key_rows(c, chunk)], finalize=finalize,
                s_buf=s_buf, max_buf=max_buf, p_buf=p_buf, acc_buf=acc_buf)


def _post_kernel(h1_ref, om_ref, od_ref, p_ref, wom_ref, wod_ref, g2_ref, wg_hbm, wu_hbm, wd_hbm,
                 gpi_ref, wpg_ref, bpg_ref, wpp_ref, gpo_ref, out_ref,
                 wg_ref, wu_ref, wd_ref, stage_in, stage_out, sem):
    _load_ffn_weights(wg_hbm, wu_hbm, wd_hbm, wg_ref, wu_ref, wd_ref, stage_in, stage_out, sem)
    h = h1_ref[...] + _dot_tn(om_ref[0], wom_ref[...]) + _dot_tn(od_ref[0], wod_ref[...])
    h = h + 0.5 * _swiglu(_rms_rows(h, g2_ref[...]).astype(BF16), wg_ref, wu_ref, wd_ref)
    z = _dot(_rms_rows(h, gpi_ref[...]).astype(BF16), wpg_ref[...]) + bpg_ref[...]
    gate = 1.0 / (1.0 + jnp.exp(-z))
    emb = _rms_rows(_dot(p_ref[...].astype(BF16), wpp_ref[...]), gpo_ref[...])
    out_ref[...] = h + gate * emb


def _resident(shape):
    nd = len(shape)
    return pl.BlockSpec(shape, lambda *_: (0,) * nd, pipeline_mode=pl.Buffered(1))


def _params(n_axes):
    return pltpu.CompilerParams(dimension_semantics=("arbitrary",) * n_axes,
                                vmem_limit_bytes=VMEM_LIMIT_BYTES)


def _ffn_weight_scratch(d_model, d_ff):
    return [pltpu.VMEM((d_model, d_ff), BF16), pltpu.VMEM((d_model, d_ff), BF16),
            pltpu.VMEM((d_ff, d_model), BF16),
            pltpu.VMEM((2, d_model // FFN_STAGE_BLOCKS, d_ff), F32),
            pltpu.VMEM((2, d_ff // FFN_STAGE_BLOCKS, d_model), F32),
            pltpu.SemaphoreType.DMA((2,))]


def _attn_scratch(chunk, width, rows):
    return [pltpu.VMEM((2, 2, chunk, width), BF16), pltpu.VMEM((2, 2, 1, width), F32),
            pltpu.VMEM((2, 2, chunk, width), BF16), pltpu.VMEM((rows, width), F32)]


def _rope_tables_t(seq):
    pos = jnp.arange(seq, dtype=F32)
    inv = ROPE_THETA ** (-jnp.arange(0, MLA_ROPE, 2, dtype=F32) / MLA_ROPE)
    ang = pos[:, None] * inv[None, :]
    ang = jnp.concatenate([ang, ang], axis=-1)
    return jnp.cos(ang).T, jnp.sin(ang).T


def _alibi_reach(sq_norms, slopes2, batch, seq):
    n_maps = 2 * DIFF_HEADS
    top = jnp.max(sq_norms[:, :, 0].reshape(batch, -1, 2 * n_maps), axis=1)
    bound = jnp.sqrt(jnp.max((top[:, :n_maps] * top[:, n_maps:]).reshape(batch, DIFF_HEADS, 2), axis=-1))
    bound = bound * 1.02
    reach = jnp.ceil((2.0 * bound + WEIGHT_CUTOFF_BITS) / slopes2[None, :])
    return jnp.minimum(reach, float(seq)).astype(jnp.int32).reshape(-1)


def _alibi_slope_parts():
    parts = []
    for h in range(DIFF_HEADS):
        s2 = np.float32(LOG2E * 2.0 ** (-8.0 * (h + 1) / DIFF_HEADS))
        bits = s2.view(np.uint32)
        hi = np.uint32((bits + np.uint32(0x7FFF) + ((bits >> np.uint32(16)) & np.uint32(1)))
                       & np.uint32(0xFFFF0000)).view(np.float32)
        parts.append((float(s2), float(hi), float(s2 - hi)))
    return tuple(parts)


def kernel(x, p, g_ffn1, w_ffn1_gate, w_ffn1_up, w_ffn1_down, g_mix, w_in, g_q_lat, w_q_up, g_kv_lat, w_kv_up, g_mla_q, g_mla_k, g_diff_q, g_diff_k, lambda_q1, lambda_k1, lambda_q2, lambda_k2, g_diff_sub, w_out, g_ffn2, w_ffn2_gate, w_ffn2_up, w_ffn2_down, g_ple_in, w_ple_gate, b_ple_gate, w_ple_proj, g_ple_out):
    batch, seq, d_model = x.shape
    depth = p.shape[0]
    tokens = batch * seq
    tm = TOKEN_TILE
    tiles_per_seq = seq // tm
    assert seq % tm == 0 and seq % POST_TOKEN_TILE == 0 and seq % MLA_Q_TILE == 0
    assert seq % (4 * MLA_KV_CHUNK) == 0 and seq % (4 * DIFF_KV_CHUNK) == 0
    assert tm == POS_BLOCK and DIFF_Q_TILE % POS_BLOCK == 0 and seq % DIFF_Q_TILE == 0

    cos_t, sin_t = _rope_tables_t(seq)
    slope_parts = _alibi_slope_parts()
    slopes2 = jnp.asarray([s2 for s2, _, _ in slope_parts], F32)
    row = lambda v: v.reshape(1, -1)
    col = lambda v: v.reshape(-1, 1)
    bf = lambda w: w.astype(BF16)

    h = x.reshape(tokens, d_model)
    for i in range(depth):
        lambda_init = 0.8 - 0.6 * math.exp(-0.3 * i)
        ple_dim = p.shape[-1]

        in_hbm = pl.BlockSpec(memory_space=pl.ANY)
        d_ff = w_ffn1_gate.shape[-1]
        tok_spec = pl.BlockSpec((tm, d_model), lambda t: (t, 0))
        head_t = lambda heads, rows, width: pl.BlockSpec(
            (1, heads, rows, width), lambda t: (t // tiles_per_seq, 0, 0, t % tiles_per_seq))
        head_r = lambda heads, width: pl.BlockSpec(
            (1, heads, tm, width), lambda t: (t // tiles_per_seq, 0, t % tiles_per_seq, 0))
        rope_spec = pl.BlockSpec((MLA_ROPE, tm), lambda t: (0, t % tiles_per_seq))
        pre_inputs = [
            (h, tok_spec),
            (row(g_ffn1[i]), None), (w_ffn1_gate[i], in_hbm), (w_ffn1_up[i], in_hbm),
            (w_ffn1_down[i], in_hbm), (row(g_mix[i]), None), (bf(w_in[i].T), None),
            (col(g_q_lat[i]), None), (bf(w_q_up[i].T), None),
            (col(g_kv_lat[i]), None), (bf(w_kv_up[i].T), None),
            (col(g_mla_q[i, :MLA_NOPE]), None), (col(g_mla_q[i, MLA_NOPE:]), None),
            (col(g_mla_k[i, :MLA_NOPE]), None), (col(g_mla_k[i, MLA_NOPE:]), None),
            (col(g_diff_q[i]), None), (col(g_diff_k[i]), None),
            (cos_t, rope_spec), (sin_t, rope_spec),
        ]
        dq_rows = 2 * DIFF_QK + AUG_ROWS
        h1, qm, km, vm, qd, kd, vd, sq_norms = pl.pallas_call(
            functools.partial(_pre_kernel, slope_parts=slope_parts, tiles_per_seq=tiles_per_seq),
            grid=(tokens // tm,),
            in_specs=[s if s is not None else _resident(a.shape) for a, s in pre_inputs],
            out_specs=[tok_spec, head_t(MLA_HEADS, QK_PAD, tm), head_r(MLA_HEADS, QK_PAD),
                       head_t(MLA_HEADS, MLA_V + ONES_ROWS, tm), head_t(DIFF_HEADS, dq_rows, 2 * tm),
                       head_r(DIFF_HEADS, DIFF_K_WIDTH), head_t(DIFF_HEADS, DIFF_V + ONES_ROWS, tm),
                       pl.BlockSpec((1, 4 * DIFF_HEADS, LANES), lambda t: (t, 0, 0))],
            out_shape=[
                jax.ShapeDtypeStruct((tokens, d_model), F32),
                jax.ShapeDtypeStruct((batch, MLA_HEADS, QK_PAD, seq), BF16),
                jax.ShapeDtypeStruct((batch, MLA_HEADS, seq, QK_PAD), BF16),
                jax.ShapeDtypeStruct((batch, MLA_HEADS, MLA_V + ONES_ROWS, seq), BF16),
                jax.ShapeDtypeStruct((batch, DIFF_HEADS, dq_rows, 2 * seq), BF16),
                jax.ShapeDtypeStruct((batch, DIFF_HEADS, seq, DIFF_K_WIDTH), BF16),
                jax.ShapeDtypeStruct((batch, DIFF_HEADS, DIFF_V + ONES_ROWS, seq), BF16),
                jax.ShapeDtypeStruct((tokens // tm, 4 * DIFF_HEADS, LANES), F32),
            ],
            scratch_shapes=_ffn_weight_scratch(d_model, d_ff),
            compiler_params=_params(1),
            name="pre",
        )(*[a for a, _ in pre_inputs])

        per_head = lambda *tail: pl.BlockSpec((1, 1) + tail, lambda b, hh: (b, hh, 0, 0))
        om = pl.pallas_call(
            _mla_attn_kernel,
            grid=(batch, MLA_HEADS),
            in_specs=[per_head(QK_PAD, seq), per_head(seq, QK_PAD), per_head(MLA_V + ONES_ROWS, seq)],
            out_specs=per_head(MLA_V, seq),
            out_shape=jax.ShapeDtypeStruct((batch, MLA_HEADS, MLA_V, seq), BF16),
            scratch_shapes=_attn_scratch(MLA_KV_CHUNK, MLA_Q_TILE, MLA_V + ONES_ROWS),
            compiler_params=_params(2),
            name="mla_attn",
        )(qm, km, vm)

        small = lambda shape: pl.BlockSpec(shape, lambda b, hh: (0,) * len(shape))
        od = pl.pallas_call(
            functools.partial(_diff_attn_kernel, lambda_init=lambda_init),
            grid=(batch, DIFF_HEADS),
            in_specs=[pl.BlockSpec(memory_space=pltpu.SMEM),
                      per_head(dq_rows, 2 * seq), per_head(seq, DIFF_K_WIDTH),
                      per_head(DIFF_V + ONES_ROWS, seq)]
                     + [small((1, DIFF_QK))] * 4 + [small((DIFF_V, 1))],
            out_specs=per_head(DIFF_V, seq),
            out_shape=jax.ShapeDtypeStruct((batch, DIFF_HEADS, DIFF_V, seq), BF16),
            scratch_shapes=_attn_scratch(DIFF_KV_CHUNK, 2 * DIFF_Q_TILE, DIFF_V + ONES_ROWS),
            compiler_params=_params(2),
            name="diff_attn",
        )(_alibi_reach(sq_norms, slopes2, batch, seq), qd, kd, vd, row(lambda_q1[i]), row(lambda_k1[i]), row(lambda_q2[i]),
          row(lambda_k2[i]), col(g_diff_sub[i]))

        mla_w = MLA_HEADS * MLA_V
        diff_w = DIFF_HEADS * DIFF_V
        tp = POST_TOKEN_TILE
        post_tiles_per_seq = seq // tp
        post_tok_spec = pl.BlockSpec((tp, d_model), lambda t: (t, 0))
        feat_t = lambda width: pl.BlockSpec(
            (1, width, tp), lambda t: (t // post_tiles_per_seq, 0, t % post_tiles_per_seq))
        post_inputs = [
            (h1, post_tok_spec),
            (om.reshape(batch, mla_w, seq), feat_t(mla_w)),
            (od.reshape(batch, diff_w, seq), feat_t(diff_w)),
            (p[i].reshape(tokens, ple_dim), pl.BlockSpec((tp, ple_dim), lambda t: (t, 0))),
            (bf(w_out[i, :mla_w]), None), (bf(w_out[i, mla_w:]), None),
            (row(g_ffn2[i]), None), (w_ffn2_gate[i], in_hbm), (w_ffn2_up[i], in_hbm),
            (w_ffn2_down[i], in_hbm),
            (row(g_ple_in[i]), None), (bf(w_ple_gate[i]), None), (row(b_ple_gate[i]), None),
            (bf(w_ple_proj[i]), None), (row(g_ple_out[i]), None),
        ]
        h = pl.pallas_call(
            _post_kernel,
            grid=(tokens // tp,),
            in_specs=[s if s is not None else _resident(a.shape) for a, s in post_inputs],
            out_specs=post_tok_spec,
            out_shape=jax.ShapeDtypeStruct((tokens, d_model), F32),
            scratch_shapes=_ffn_weight_scratch(d_model, d_ff),
            compiler_params=_params(1),
            name="post",
        )(*[a for a, _ in post_inputs])

    return h.reshape(batch, seq, d_model)
```

```python
import functools
import math

import jax
import jax.numpy as jnp
import numpy as np
from jax import lax
from jax.experimental import pallas as pl
from jax.experimental.pallas import tpu as pltpu

F32 = jnp.float32
BF16 = jnp.bfloat16

MLA_HEADS = 8
MLA_NOPE = 64
MLA_ROPE = 32
MLA_QK = MLA_NOPE + MLA_ROPE
MLA_V = 64
MLA_Q_RANK = 256
MLA_KV_RANK = 128
ROPE_THETA = 10000.0
DIFF_HEADS = 4
DIFF_QK = 64
DIFF_V = 2 * DIFF_QK
EPS = 1e-6

QK_PAD = 128
ONES_ROWS = 16
AUG_ROWS = 16
DIFF_K_WIDTH = 256
POS_BLOCK = 256
NEG_BIG = -1e30
WEIGHT_CUTOFF_BITS = 130.0
LANES = 128
LOG2E = math.log2(math.e)

VMEM_LIMIT_BYTES = 56 * 1024 * 1024
TOKEN_TILE = 256
POST_TOKEN_TILE = 512
FFN_STAGE_BLOCKS = 8
MLA_Q_TILE = 1024
DIFF_Q_TILE = 512
MLA_KV_CHUNK = 512
DIFF_KV_CHUNK = 512


def _dot(a, b):
    return jnp.dot(a, b, preferred_element_type=F32)


def _dot_nt(a, b):
    return lax.dot_general(a, b, (((1,), (1,)), ((), ())), preferred_element_type=F32)


def _dot_tn(a, b):
    return lax.dot_general(a, b, (((0,), (0,)), ((), ())), preferred_element_type=F32)


def _rms_rows(x, g_row):
    ms = jnp.mean(x * x, axis=-1, keepdims=True)
    return x * lax.rsqrt(ms + EPS) * g_row


def _rms_cols(xt, g_col):
    ms = jnp.mean(xt * xt, axis=0, keepdims=True)
    return xt * lax.rsqrt(ms + EPS) * g_col


def _swiglu(xn, wg_ref, wu_ref, wd_ref):
    gate = _dot(xn, wg_ref[...])
    up = _dot(xn, wu_ref[...])
    act = (gate / (1.0 + jnp.exp(-gate))) * up
    return _dot(act.astype(BF16), wd_ref[...])


def _load_as_bf16(src_hbm, dst, stage, sem):
    rows = stage.shape[1]
    n_blocks = src_hbm.shape[0] // rows
    assert n_blocks * rows == src_hbm.shape[0] and stage.shape[2] == src_hbm.shape[1]

    def block_copy(c):
        return pltpu.make_async_copy(src_hbm.at[pl.ds(c * rows, rows), :], stage.at[c % 2], sem.at[c % 2])

    block_copy(0).start()
    for c in range(n_blocks):
        if c + 1 < n_blocks:
            block_copy(c + 1).start()
        block_copy(c).wait()
        dst[pl.ds(c * rows, rows), :] = stage[c % 2].astype(BF16)


def _load_ffn_weights(wg_hbm, wu_hbm, wd_hbm, wg_ref, wu_ref, wd_ref, stage_in, stage_out, sem):
    @pl.when(pl.program_id(0) == 0)
    def _():
        _load_as_bf16(wg_hbm, wg_ref, stage_in, sem)
        _load_as_bf16(wu_hbm, wu_ref, stage_in, sem)
        _load_as_bf16(wd_hbm, wd_ref, stage_out, sem)


def _rope_cols(xt, cos, sin):
    half = MLA_ROPE // 2
    rot = jnp.concatenate([-xt[half:], xt[:half]], axis=0)
    return xt * cos + rot * sin


def _pre_kernel(x_ref, g1_ref, wg_hbm, wu_hbm, wd_hbm, gmix_ref, win_t_ref,
                gql_ref, wqu_t_ref, gkvl_ref, wkvu_t_ref,
                gqn_ref, gqr_ref, gkn_ref, gkr_ref, gdq_ref, gdk_ref, cos_ref, sin_ref,
                h1_ref, qm_ref, km_ref, vm_ref, qd_ref, kd_ref, vd_ref, norm_ref,
                wg_ref, wu_ref, wd_ref, stage_in, stage_out, sem,
                *, slope_parts, tiles_per_seq):
    _load_ffn_weights(wg_hbm, wu_hbm, wd_hbm, wg_ref, wu_ref, wd_ref, stage_in, stage_out, sem)
    t = x_ref.shape[0]
    x = x_ref[...]
    h1 = x + 0.5 * _swiglu(_rms_rows(x, g1_ref[...]).astype(BF16), wg_ref, wu_ref, wd_ref)
    h1_ref[...] = h1

    hn = _rms_rows(h1, gmix_ref[...]).astype(BF16)
    ut = _dot_nt(win_t_ref[...], hn)
    o_kv = MLA_Q_RANK
    o_kr = o_kv + MLA_KV_RANK
    o_qd = o_kr + MLA_ROPE
    o_kd = o_qd + DIFF_HEADS * 2 * DIFF_QK
    o_vd = o_kd + DIFF_HEADS * 2 * DIFF_QK

    cos = cos_ref[...]
    sin = sin_ref[...]
    ones = jnp.ones((ONES_ROWS, t), BF16)
    zeros_pad = jnp.zeros((QK_PAD - MLA_QK, t), F32)

    q_scale = MLA_QK ** -0.5 * LOG2E
    qln = _rms_cols(ut[0:o_kv], gql_ref[...]).astype(BF16)
    qt = _dot(wqu_t_ref[...], qln)
    kvn = _rms_cols(ut[o_kv:o_kr], gkvl_ref[...]).astype(BF16)
    kvt = _dot(wkvu_t_ref[...], kvn)
    k_rot = _rope_cols(_rms_cols(ut[o_kr:o_qd], gkr_ref[...]), cos, sin)
    for h in range(MLA_HEADS):
        qh = qt[h * MLA_QK:(h + 1) * MLA_QK]
        q_nope = _rms_cols(qh[:MLA_NOPE], gqn_ref[...])
        q_rot = _rope_cols(_rms_cols(qh[MLA_NOPE:], gqr_ref[...]), cos, sin)
        qm_ref[0, h, 0:MLA_NOPE, :] = (q_nope * q_scale).astype(BF16)
        qm_ref[0, h, MLA_NOPE:MLA_QK, :] = (q_rot * q_scale).astype(BF16)
        qm_ref[0, h, MLA_QK:QK_PAD, :] = zeros_pad.astype(BF16)
        kvh = kvt[h * (MLA_NOPE + MLA_V):(h + 1) * (MLA_NOPE + MLA_V)]
        k_nope = _rms_cols(kvh[:MLA_NOPE], gkn_ref[...])
        k_full = jnp.concatenate([k_nope, k_rot, zeros_pad], axis=0)
        km_ref[0, h] = k_full.T.astype(BF16)
        vm_ref[0, h, 0:MLA_V, :] = kvh[MLA_NOPE:].astype(BF16)
        vm_ref[0, h, MLA_V:MLA_V + ONES_ROWS, :] = ones

    d_scale = DIFF_QK ** -0.5 * LOG2E
    zero_q = jnp.zeros((DIFF_QK, t), BF16)
    n_maps = 2 * DIFF_HEADS

    assert t == POS_BLOCK
    blk = (pl.program_id(0) % tiles_per_seq).astype(F32)

    def pick(index, values):
        out = jnp.zeros(index.shape, F32)
        for n, v in enumerate(values):
            out = jnp.where(index == n, v, out)
        return out

    def q_aug(hi, lo):
        r = lax.broadcasted_iota(jnp.int32, (AUG_ROWS, 2 * t), 0)
        rem = (lax.broadcasted_iota(jnp.int32, (AUG_ROWS, 2 * t), 1) % t).astype(F32)
        return pick(r, [blk, blk, rem, rem, POS_BLOCK * hi, POS_BLOCK * lo, hi, lo]).astype(BF16)

    def k_aug(hi, lo):
        width = DIFF_K_WIDTH - 2 * DIFF_QK
        c = lax.broadcasted_iota(jnp.int32, (t, width), 1)
        rem = lax.broadcasted_iota(jnp.int32, (t, width), 0).astype(F32)
        return pick(c, [-POS_BLOCK * hi, -POS_BLOCK * lo, -hi, -lo, blk, blk, rem, rem]).astype(BF16)

    def put_max_sq_norm(row, xt):
        sq = jnp.max(jnp.sum(xt * xt, axis=0, keepdims=True), axis=1, keepdims=True)
        norm_ref[0, row:row + 1, :] = jnp.broadcast_to(sq, (1, norm_ref.shape[2]))

    for h in range(DIFF_HEADS):
        k_maps = []
        for j in range(2):
            r = (2 * h + j) * DIFF_QK
            qj = _rms_cols(ut[o_qd + r:o_qd + r + DIFF_QK], gdq_ref[...]) * d_scale
            kj = _rms_cols(ut[o_kd + r:o_kd + r + DIFF_QK], gdk_ref[...])
            put_max_sq_norm(2 * h + j, qj)
            put_max_sq_norm(n_maps + 2 * h + j, kj)
            qd_ref[0, h, j * DIFF_QK:(j + 1) * DIFF_QK, j * t:(j + 1) * t] = qj.astype(BF16)
            qd_ref[0, h, (1 - j) * DIFF_QK:(2 - j) * DIFF_QK, j * t:(j + 1) * t] = zero_q
            k_maps.append(kj)
        qd_ref[0, h, 2 * DIFF_QK:2 * DIFF_QK + AUG_ROWS, :] = q_aug(*slope_parts[h][1:])
        k12 = jnp.concatenate(k_maps, axis=0)
        kd_ref[0, h, :, 0:2 * DIFF_QK] = k12.T.astype(BF16)
        kd_ref[0, h, :, 2 * DIFF_QK:DIFF_K_WIDTH] = k_aug(*slope_parts[h][1:])
        vd_ref[0, h, 0:DIFF_V, :] = ut[o_vd + h * DIFF_V:o_vd + (h + 1) * DIFF_V].astype(BF16)
        vd_ref[0, h, DIFF_V:DIFF_V + ONES_ROWS, :] = ones


def _key_rows(c, size):
    return pl.ds(pl.multiple_of(c * size, size), size)


def _store_scores(s_buf, max_buf, slot, scores, max_before_rounding=False):
    sb = scores.astype(BF16)
    s_buf[slot] = sb
    if max_before_rounding:
        max_buf[slot] = jnp.max(scores, axis=0, keepdims=True).astype(BF16).astype(F32)
    else:
        max_buf[slot] = jnp.max(sb, axis=0, keepdims=True).astype(F32)


def _sweep_head(*, n_tiles, width, window, first_chunk, issue_scores, v_chunk, finalize,
                s_buf, max_buf, p_buf, acc_buf):
    def chunk_of(qi, i, k):
        lo, count = window(qi)
        return lo + (first_chunk(qi) - lo + 2 * i + k) % count

    def step(qi, i, parity, carry, first=False, last=False):
        m, alphas = carry
        cur = [(s_buf[parity, k], max_buf[parity, k]) for k in range(2)]

        nxt_qi, nxt_i = (qi + 1, 0) if last else (qi, i + 1)
        nxt_qi = jnp.minimum(nxt_qi, n_tiles - 1)
        for k in range(2):
            issue_scores(nxt_qi, chunk_of(nxt_qi, nxt_i, k), last and k == 0, (1 - parity, k))

        if first:
            prv_qi = jnp.maximum(qi - 1, 0)
            prv_i = window(prv_qi)[1] // 2 - 1
        else:
            prv_qi, prv_i = qi, i - 1
        for k in range(2):
            acc_buf[...] = (acc_buf[...] * alphas[k]
                            + _dot(v_chunk(chunk_of(prv_qi, prv_i, k)), p_buf[1 - parity, k]))

        if first:
            m = jnp.full((1, width), NEG_BIG, F32)
        alphas = []
        for k, (sb, chunk_max) in enumerate(cur):
            m_new = jnp.maximum(m, chunk_max)
            p_buf[parity, k] = jnp.exp2(sb - m_new.astype(BF16))
            alphas.append(jnp.exp2(m - m_new))
            m = m_new
        return m, tuple(alphas)

    def tile(qi, carry):
        def two_steps(j, cr):
            cr = step(qi, 2 * j + 1, 1, cr)
            return step(qi, 2 * j + 2, 0, cr)
        n_pairs = window(qi)[1] // 2
        carry = lax.fori_loop(0, n_pairs // 2 - 1, two_steps, carry)
        carry = step(qi, n_pairs - 1, 1, carry, last=True)
        carry = step(qi + 1, 0, 0, carry, first=True)
        finalize(qi)
        return carry

    p_buf[...] = jnp.zeros(p_buf.shape, p_buf.dtype)
    acc_buf[...] = jnp.zeros(acc_buf.shape, acc_buf.dtype)
    for k in range(2):
        issue_scores(0, chunk_of(0, 0, k), k == 0, (0, k))
    zero_row = jnp.zeros((1, width), F32)
    carry = (zero_row, (zero_row, zero_row))
    carry = step(0, 0, 0, carry, first=True)
    lax.fori_loop(0, n_tiles, tile, carry)


def _mla_attn_kernel(q_ref, k_ref, v_ref, o_ref, s_buf, max_buf, p_buf, acc_buf):
    chunk, tq = s_buf.shape[2:]
    seq = k_ref.shape[2]

    def tile_cols(qi):
        return pl.ds(pl.multiple_of(qi * tq, tq), tq)

    def issue_scores(qi, c, is_first_chunk, slot):
        _store_scores(s_buf, max_buf, slot,
                      _dot(k_ref[0, 0, _key_rows(c, chunk), :], q_ref[0, 0, :, tile_cols(qi)]))

    def finalize(qi):
        o_ref[0, 0, :, tile_cols(qi)] = (
            acc_buf[0:MLA_V, :] / acc_buf[MLA_V:MLA_V + 1, :]).astype(BF16)

    _sweep_head(n_tiles=seq // tq, width=tq, window=lambda qi: (0, seq // chunk),
                first_chunk=lambda qi: 0, issue_scores=issue_scores,
                v_chunk=lambda c: v_ref[0, 0, :, _key_rows(c, chunk)], finalize=finalize,
                s_buf=s_buf, max_buf=max_buf, p_buf=p_buf, acc_buf=acc_buf)


def _diff_attn_kernel(reach_ref, q_ref, k_ref, v_ref, lq1_ref, lk1_ref, lq2_ref, lk2_ref, gsub_ref,
                      o_ref, s_buf, max_buf, p_buf, acc_buf, *, lambda_init):
    chunk, tq = s_buf.shape[2], s_buf.shape[3] // 2
    seq = k_ref.shape[2]
    n_chunks = seq // chunk
    assert chunk % tq == 0 and n_chunks % 4 == 0
    first_chunk = lambda qi: (qi * tq) // chunk
    reach = reach_ref[pl.program_id(0) * pl.num_programs(1) + pl.program_id(1)]

    def window(qi):
        lo = jnp.maximum(qi * tq - reach, 0) // chunk
        hi = jnp.minimum(qi * tq + tq - 1 + reach, seq - 1) // chunk
        count = jnp.minimum((hi - lo + 4) // 4 * 4, n_chunks)
        return jnp.minimum(lo, n_chunks - count), count

    lam = (jnp.exp(jnp.sum(lq1_ref[...] * lk1_ref[...], axis=-1, keepdims=True))
           - jnp.exp(jnp.sum(lq2_ref[...] * lk2_ref[...], axis=-1, keepdims=True)) + lambda_init)
    zero_rows = jnp.zeros((DIFF_K_WIDTH - 2 * DIFF_QK - AUG_ROWS, 2 * tq), BF16)

    def issue_scores(qi, c, is_first_chunk, slot):
        qa = q_ref[0, 0, :, pl.ds(pl.multiple_of(qi * 2 * tq, 2 * tq), 2 * tq)]
        kc = k_ref[0, 0, _key_rows(c, chunk), :]

        def scores(sign):
            aug = (qa[2 * DIFF_QK:].astype(F32) * sign).astype(BF16)
            return _dot(kc, jnp.concatenate([qa[:2 * DIFF_QK], aug, zero_rows], axis=0))

        if is_first_chunk:
            st = jnp.minimum(scores(1.0), scores(-1.0))
        else:
            st = scores(jnp.where(c < first_chunk(qi), 1.0, -1.0).astype(F32))
        _store_scores(s_buf, max_buf, slot, st, max_before_rounding=True)

    def finalize(qi):
        o_all = acc_buf[0:DIFF_V, :] / acc_buf[DIFF_V:DIFF_V + 1, :]
        sub = POS_BLOCK
        o = jnp.concatenate(
            [o_all[:, 2 * j * sub:(2 * j + 1) * sub] - lam * o_all[:, (2 * j + 1) * sub:(2 * j + 2) * sub]
             for j in range(tq // sub)], axis=1)
        o_ref[0, 0, :, pl.ds(pl.multiple_of(qi * tq, tq), tq)] = (
            _rms_cols(o, gsub_ref[...]) * (1.0 - lambda_init)).astype(BF16)

    _sweep_head(n_tiles=seq // tq, width=2 * tq, window=window,
                first_chunk=first_chunk, issue_scores=issue_scores,
                v_chunk=lambda c: v_ref[0, 0, :, _key_rows(c, chunk)], finalize=finalize,
                s_buf=s_buf, max_buf=max_buf, p_buf=p_buf, acc_buf=acc_buf)


def _post_kernel(h1_ref, om_ref, od_ref, p_ref, wom_ref, wod_ref, g2_ref, wg_hbm, wu_hbm, wd_hbm,
                 gpi_ref, wpg_ref, bpg_ref, wpp_ref, gpo_ref, out_ref,
                 wg_ref, wu_ref, wd_ref, stage_in, stage_out, sem):
    _load_ffn_weights(wg_hbm, wu_hbm, wd_hbm, wg_ref, wu_ref, wd_ref, stage_in, stage_out, sem)
    h = h1_ref[...] + _dot_tn(om_ref[0], wom_ref[...]) + _dot_tn(od_ref[0], wod_ref[...])
    h = h + 0.5 * _swiglu(_rms_rows(h, g2_ref[...]).astype(BF16), wg_ref, wu_ref, wd_ref)
    z = _dot(_rms_rows(h, gpi_ref[...]).astype(BF16), wpg_ref[...]) + bpg_ref[...]
    gate = 1.0 / (1.0 + jnp.exp(-z))
    emb = _rms_rows(_dot(p_ref[...].astype(BF16), wpp_ref[...]), gpo_ref[...])
    out_ref[...] = h + gate * emb


def _resident(shape):
    nd = len(shape)
    return pl.BlockSpec(shape, lambda *_: (0,) * nd, pipeline_mode=pl.Buffered(1))


def _params(n_axes):
    return pltpu.CompilerParams(dimension_semantics=("arbitrary",) * n_axes,
                                vmem_limit_bytes=VMEM_LIMIT_BYTES)


def _ffn_weight_scratch(d_model, d_ff):
    return [pltpu.VMEM((d_model, d_ff), BF16), pltpu.VMEM((d_model, d_ff), BF16),
            pltpu.VMEM((d_ff, d_model), BF16),
            pltpu.VMEM((2, d_model // FFN_STAGE_BLOCKS, d_ff), F32),
            pltpu.VMEM((2, d_ff // FFN_STAGE_BLOCKS, d_model), F32),
            pltpu.SemaphoreType.DMA((2,))]


def _attn_scratch(chunk, width, rows):
    return [pltpu.VMEM((2, 2, chunk, width), BF16), pltpu.VMEM((2, 2, 1, width), F32),
            pltpu.VMEM((2, 2, chunk, width), BF16), pltpu.VMEM((rows, width), F32)]


def _rope_tables_t(seq):
    pos = jnp.arange(seq, dtype=F32)
    inv = ROPE_THETA ** (-jnp.arange(0, MLA_ROPE, 2, dtype=F32) / MLA_ROPE)
    ang = pos[:, None] * inv[None, :]
    ang = jnp.concatenate([ang, ang], axis=-1)
    return jnp.cos(ang).T, jnp.sin(ang).T


def _alibi_reach(sq_norms, slopes2, batch, seq):
    n_maps = 2 * DIFF_HEADS
    top = jnp.max(sq_norms[:, :, 0].reshape(batch, -1, 2 * n_maps), axis=1)
    bound = jnp.sqrt(jnp.max((top[:, :n_maps] * top[:, n_maps:]).reshape(batch, DIFF_HEADS, 2), axis=-1))
    bound = bound * 1.02
    reach = jnp.ceil((2.0 * bound + WEIGHT_CUTOFF_BITS) / slopes2[None, :])
    return jnp.minimum(reach, float(seq)).astype(jnp.int32).reshape(-1)


def _alibi_slope_parts():
    parts = []
    for h in range(DIFF_HEADS):
        s2 = np.float32(LOG2E * 2.0 ** (-8.0 * (h + 1) / DIFF_HEADS))
        bits = s2.view(np.uint32)
        hi = np.uint32((bits + np.uint32(0x7FFF) + ((bits >> np.uint32(16)) & np.uint32(1)))
                       & np.uint32(0xFFFF0000)).view(np.float32)
        parts.append((float(s2), float(hi), float(s2 - hi)))
    return tuple(parts)


def kernel(x, p, g_ffn1, w_ffn1_gate, w_ffn1_up, w_ffn1_down, g_mix, w_in, g_q_lat, w_q_up, g_kv_lat, w_kv_up, g_mla_q, g_mla_k, g_diff_q, g_diff_k, lambda_q1, lambda_k1, lambda_q2, lambda_k2, g_diff_sub, w_out, g_ffn2, w_ffn2_gate, w_ffn2_up, w_ffn2_down, g_ple_in, w_ple_gate, b_ple_gate, w_ple_proj, g_ple_out):
    batch, seq, d_model = x.shape
    depth = p.shape[0]
    tokens = batch * seq
    tm = TOKEN_TILE
    tiles_per_seq = seq // tm
    assert seq % tm == 0 and seq % POST_TOKEN_TILE == 0 and seq % MLA_Q_TILE == 0
    assert seq % (4 * MLA_KV_CHUNK) == 0 and seq % (4 * DIFF_KV_CHUNK) == 0
    assert tm == POS_BLOCK and DIFF_Q_TILE % POS_BLOCK == 0 and seq % DIFF_Q_TILE == 0

    cos_t, sin_t = _rope_tables_t(seq)
    slope_parts = _alibi_slope_parts()
    slopes2 = jnp.asarray([s2 for s2, _, _ in slope_parts], F32)
    row = lambda v: v.reshape(1, -1)
    col = lambda v: v.reshape(-1, 1)
    bf = lambda w: w.astype(BF16)

    h = x.reshape(tokens, d_model)
    for i in range(depth):
        lambda_init = 0.8 - 0.6 * math.exp(-0.3 * i)
        ple_dim = p.shape[-1]

        in_hbm = pl.BlockSpec(memory_space=pl.ANY)
        d_ff = w_ffn1_gate.shape[-1]
        tok_spec = pl.BlockSpec((tm, d_model), lambda t: (t, 0))
        head_t = lambda heads, rows, width: pl.BlockSpec(
            (1, heads, rows, width), lambda t: (t // tiles_per_seq, 0, 0, t % tiles_per_seq))
        head_r = lambda heads, width: pl.BlockSpec(
            (1, heads, tm, width), lambda t: (t // tiles_per_seq, 0, t % tiles_per_seq, 0))
        rope_spec = pl.BlockSpec((MLA_ROPE, tm), lambda t: (0, t % tiles_per_seq))
        pre_inputs = [
            (h, tok_spec),
            (row(g_ffn1[i]), None), (w_ffn1_gate[i], in_hbm), (w_ffn1_up[i], in_hbm),
            (w_ffn1_down[i], in_hbm), (row(g_mix[i]), None), (bf(w_in[i].T), None),
            (col(g_q_lat[i]), None), (bf(w_q_up[i].T), None),
            (col(g_kv_lat[i]), None), (bf(w_kv_up[i].T), None),
            (col(g_mla_q[i, :MLA_NOPE]), None), (col(g_mla_q[i, MLA_NOPE:]), None),
            (col(g_mla_k[i, :MLA_NOPE]), None), (col(g_mla_k[i, MLA_NOPE:]), None),
            (col(g_diff_q[i]), None), (col(g_diff_k[i]), None),
            (cos_t, rope_spec), (sin_t, rope_spec),
        ]
        dq_rows = 2 * DIFF_QK + AUG_ROWS
        h1, qm, km, vm, qd, kd, vd, sq_norms = pl.pallas_call(
            functools.partial(_pre_kernel, slope_parts=slope_parts, tiles_per_seq=tiles_per_seq),
            grid=(tokens // tm,),
            in_specs=[s if s is not None else _resident(a.shape) for a, s in pre_inputs],
            out_specs=[tok_spec, head_t(MLA_HEADS, QK_PAD, tm), head_r(MLA_HEADS, QK_PAD),
                       head_t(MLA_HEADS, MLA_V + ONES_ROWS, tm), head_t(DIFF_HEADS, dq_rows, 2 * tm),
                       head_r(DIFF_HEADS, DIFF_K_WIDTH), head_t(DIFF_HEADS, DIFF_V + ONES_ROWS, tm),
                       pl.BlockSpec((1, 4 * DIFF_HEADS, LANES), lambda t: (t, 0, 0))],
            out_shape=[
                jax.ShapeDtypeStruct((tokens, d_model), F32),
                jax.ShapeDtypeStruct((batch, MLA_HEADS, QK_PAD, seq), BF16),
                jax.ShapeDtypeStruct((batch, MLA_HEADS, seq, QK_PAD), BF16),
                jax.ShapeDtypeStruct((batch, MLA_HEADS, MLA_V + ONES_ROWS, seq), BF16),
                jax.ShapeDtypeStruct((batch, DIFF_HEADS, dq_rows, 2 * seq), BF16),
                jax.ShapeDtypeStruct((batch, DIFF_HEADS, seq, DIFF_K_WIDTH), BF16),
                jax.ShapeDtypeStruct((batch, DIFF_HEADS, DIFF_V + ONES_ROWS, seq), BF16),
                jax.ShapeDtypeStruct((tokens // tm, 4 * DIFF_HEADS, LANES), F32),
            ],
            scratch_shapes=_ffn_weight_scratch(d_model, d_ff),
            compiler_params=_params(1),
            name="pre",
        )(*[a for a, _ in pre_inputs])

        per_head = lambda *tail: pl.BlockSpec((1, 1) + tail, lambda b, hh: (b, hh, 0, 0))
        om = pl.pallas_call(
            _mla_attn_kernel,
            grid=(batch, MLA_HEADS),
            in_specs=[per_head(QK_PAD, seq), per_head(seq, QK_PAD), per_head(MLA_V + ONES_ROWS, seq)],
            out_specs=per_head(MLA_V, seq),
            out_shape=jax.ShapeDtypeStruct((batch, MLA_HEADS, MLA_V, seq), BF16),
            scratch_shapes=_attn_scratch(MLA_KV_CHUNK, MLA_Q_TILE, MLA_V + ONES_ROWS),
            compiler_params=_params(2),
            name="mla_attn",
        )(qm, km, vm)

        small = lambda shape: pl.BlockSpec(shape, lambda b, hh: (0,) * len(shape))
        od = pl.pallas_call(
            functools.partial(_diff_attn_kernel, lambda_init=lambda_init),
            grid=(batch, DIFF_HEADS),
            in_specs=[pl.BlockSpec(memory_space=pltpu.SMEM),
                      per_head(dq_rows, 2 * seq), per_head(seq, DIFF_K_WIDTH),
                      per_head(DIFF_V + ONES_ROWS, seq)]
                     + [small((1, DIFF_QK))] * 4 + [small((DIFF_V, 1))],
            out_specs=per_head(DIFF_V, seq),
            out_shape=jax.ShapeDtypeStruct((batch, DIFF_HEADS, DIFF_V, seq), BF16),
            scratch_shapes=_attn_scratch(DIFF_KV_CHUNK, 2 * DIFF_Q_TILE, DIFF_V + ONES_ROWS),
            compiler_params=_params(2),
            name="diff_attn",
        )(_alibi_reach(sq_norms, slopes2, batch, seq), qd, kd, vd, row(lambda_q1[i]), row(lambda_k1[i]), row(lambda_q2[i]),
          row(lambda_k2[i]), col(g_diff_sub[i]))

        mla_w = MLA_HEADS * MLA_V
        diff_w = DIFF_HEADS * DIFF_V
        tp = POST_TOKEN_TILE
        post_tiles_per_seq = seq // tp
        post_tok_spec = pl.BlockSpec((tp, d_model), lambda t: (t, 0))
        feat_t = lambda width: pl.BlockSpec(
            (1, width, tp), lambda t: (t // post_tiles_per_seq, 0, t % post_tiles_per_seq))
        post_inputs = [
            (h1, post_tok_spec),
            (om.reshape(batch, mla_w, seq), feat_t(mla_w)),
            (od.reshape(batch, diff_w, seq), feat_t(diff_w)),
            (p[i].reshape(tokens, ple_dim), pl.BlockSpec((tp, ple_dim), lambda t: (t, 0))),
            (bf(w_out[i, :mla_w]), None), (bf(w_out[i, mla_w:]), None),
            (row(g_ffn2[i]), None), (w_ffn2_gate[i], in_hbm), (w_ffn2_up[i], in_hbm),
            (w_ffn2_down[i], in_hbm),
            (row(g_ple_in[i]), None), (bf(w_ple_gate[i]), None), (row(b_ple_gate[i]), None),
            (bf(w_ple_proj[i]), None), (row(g_ple_out[i]), None),
        ]
        h = pl.pallas_call(
            _post_kernel,
            grid=(tokens // tp,),
            in_specs=[s if s is not None else _resident(a.shape) for a, s in post_inputs],
            out_specs=post_tok_spec,
            out_shape=jax.ShapeDtypeStruct((tokens, d_model), F32),
            scratch_shapes=_ffn_weight_scratch(d_model, d_ff),
            compiler_params=_params(1),
            name="post",
        )(*[a for a, _ in post_inputs])

    return h.reshape(batch, seq, d_model)
```

```python
import functools
import math

import jax
import jax.numpy as jnp
import numpy as np
from jax import lax
from jax.experimental import pallas as pl
from jax.experimental.pallas import tpu as pltpu

F32 = jnp.float32
BF16 = jnp.bfloat16

MLA_HEADS = 8
MLA_NOPE = 64
MLA_ROPE = 32
MLA_QK = MLA_NOPE + MLA_ROPE
MLA_V = 64
MLA_Q_RANK = 256
MLA_KV_RANK = 128
ROPE_THETA = 10000.0
DIFF_HEADS = 4
DIFF_QK = 64
DIFF_V = 2 * DIFF_QK
EPS = 1e-6

QK_PAD = 128
ONES_ROWS = 16
AUG_ROWS = 16
DIFF_K_WIDTH = 256
POS_BLOCK = 256
NEG_BIG = -1e30
WEIGHT_CUTOFF_BITS = 130.0
LANES = 128
LOG2E = math.log2(math.e)

VMEM_LIMIT_BYTES = 56 * 1024 * 1024
TOKEN_TILE = 256
POST_TOKEN_TILE = 512
FFN_STAGE_BLOCKS = 8
MLA_Q_TILE = 1024
DIFF_Q_TILE = 512
MLA_KV_CHUNK = 512
DIFF_KV_CHUNK = 512


def _dot(a, b):
    return jnp.dot(a, b, preferred_element_type=F32)


def _dot_nt(a, b):
    return lax.dot_general(a, b, (((1,), (1,)), ((), ())), preferred_element_type=F32)


def _dot_tn(a, b):
    return lax.dot_general(a, b, (((0,), (0,)), ((), ())), preferred_element_type=F32)


def _rms_rows(x, g_row):
    ms = jnp.mean(x * x, axis=-1, keepdims=True)
    return x * lax.rsqrt(ms + EPS) * g_row


def _rms_cols(xt, g_col):
    ms = jnp.mean(xt * xt, axis=0, keepdims=True)
    return xt * lax.rsqrt(ms + EPS) * g_col


def _swiglu(xn, wg_ref, wu_ref, wd_ref):
    gate = _dot(xn, wg_ref[...])
    up = _dot(xn, wu_ref[...])
    act = (gate / (1.0 + jnp.exp(-gate))) * up
    return _dot(act.astype(BF16), wd_ref[...])


def _load_as_bf16(src_hbm, dst, stage, sem):
    rows = stage.shape[1]
    n_blocks = src_hbm.shape[0] // rows
    assert n_blocks * rows == src_hbm.shape[0] and stage.shape[2] == src_hbm.shape[1]

    def block_copy(c):
        return pltpu.make_async_copy(src_hbm.at[pl.ds(c * rows, rows), :], stage.at[c % 2], sem.at[c % 2])

    block_copy(0).start()
    for c in range(n_blocks):
        if c + 1 < n_blocks:
            block_copy(c + 1).start()
        block_copy(c).wait()
        dst[pl.ds(c * rows, rows), :] = stage[c % 2].astype(BF16)


def _load_ffn_weights(wg_hbm, wu_hbm, wd_hbm, wg_ref, wu_ref, wd_ref, stage_in, stage_out, sem):
    @pl.when(pl.program_id(0) == 0)
    def _():
        _load_as_bf16(wg_hbm, wg_ref, stage_in, sem)
        _load_as_bf16(wu_hbm, wu_ref, stage_in, sem)
        _load_as_bf16(wd_hbm, wd_ref, stage_out, sem)


def _rope_cols(xt, cos, sin):
    half = MLA_ROPE // 2
    rot = jnp.concatenate([-xt[half:], xt[:half]], axis=0)
    return xt * cos + rot * sin


def _pre_kernel(x_ref, g1_ref, wg_hbm, wu_hbm, wd_hbm, gmix_ref, win_t_ref,
                gql_ref, wqu_t_ref, gkvl_ref, wkvu_t_ref,
                gqn_ref, gqr_ref, gkn_ref, gkr_ref, gdq_ref, gdk_ref, cos_ref, sin_ref,
                h1_ref, qm_ref, km_ref, vm_ref, qd_ref, kd_ref, vd_ref, norm_ref,
                wg_ref, wu_ref, wd_ref, stage_in, stage_out, sem,
                *, slope_parts, tiles_per_seq):
    _load_ffn_weights(wg_hbm, wu_hbm, wd_hbm, wg_ref, wu_ref, wd_ref, stage_in, stage_out, sem)
    t = x_ref.shape[0]
    x = x_ref[...]
    h1 = x + 0.5 * _swiglu(_rms_rows(x, g1_ref[...]).astype(BF16), wg_ref, wu_ref, wd_ref)
    h1_ref[...] = h1

    hn = _rms_rows(h1, gmix_ref[...]).astype(BF16)
    ut = _dot_nt(win_t_ref[...], hn)
    o_kv = MLA_Q_RANK
    o_kr = o_kv + MLA_KV_RANK
    o_qd = o_kr + MLA_ROPE
    o_kd = o_qd + DIFF_HEADS * 2 * DIFF_QK
    o_vd = o_kd + DIFF_HEADS * 2 * DIFF_QK

    cos = cos_ref[...]
    sin = sin_ref[...]
    ones = jnp.ones((ONES_ROWS, t), BF16)
    zeros_pad = jnp.zeros((QK_PAD - MLA_QK, t), F32)

    q_scale = MLA_QK ** -0.5 * LOG2E
    qln = _rms_cols(ut[0:o_kv], gql_ref[...]).astype(BF16)
    qt = _dot(wqu_t_ref[...], qln)
    kvn = _rms_cols(ut[o_kv:o_kr], gkvl_ref[...]).astype(BF16)
    kvt = _dot(wkvu_t_ref[...], kvn)
    k_rot = _rope_cols(_rms_cols(ut[o_kr:o_qd], gkr_ref[...]), cos, sin)
    for h in range(MLA_HEADS):
        qh = qt[h * MLA_QK:(h + 1) * MLA_QK]
        q_nope = _rms_cols(qh[:MLA_NOPE], gqn_ref[...])
        q_rot = _rope_cols(_rms_cols(qh[MLA_NOPE:], gqr_ref[...]), cos, sin)
        qm_ref[0, h, 0:MLA_NOPE, :] = (q_nope * q_scale).astype(BF16)
        qm_ref[0, h, MLA_NOPE:MLA_QK, :] = (q_rot * q_scale).astype(BF16)
        qm_ref[0, h, MLA_QK:QK_PAD, :] = zeros_pad.astype(BF16)
        kvh = kvt[h * (MLA_NOPE + MLA_V):(h + 1) * (MLA_NOPE + MLA_V)]
        k_nope = _rms_cols(kvh[:MLA_NOPE], gkn_ref[...])
        k_full = jnp.concatenate([k_nope, k_rot, zeros_pad], axis=0)
        km_ref[0, h] = k_full.T.astype(BF16)
        vm_ref[0, h, 0:MLA_V, :] = kvh[MLA_NOPE:].astype(BF16)
        vm_ref[0, h, MLA_V:MLA_V + ONES_ROWS, :] = ones

    d_scale = DIFF_QK ** -0.5 * LOG2E
    zero_q = jnp.zeros((DIFF_QK, t), BF16)
    n_maps = 2 * DIFF_HEADS

    assert t == POS_BLOCK
    blk = (pl.program_id(0) % tiles_per_seq).astype(F32)

    def pick(index, values):
        out = jnp.zeros(index.shape, F32)
        for n, v in enumerate(values):
            out = jnp.where(index == n, v, out)
        return out

    def q_aug(hi, lo):
        r = lax.broadcasted_iota(jnp.int32, (AUG_ROWS, 2 * t), 0)
        rem = (lax.broadcasted_iota(jnp.int32, (AUG_ROWS, 2 * t), 1) % t).astype(F32)
        return pick(r, [blk, blk, rem, rem, POS_BLOCK * hi, POS_BLOCK * lo, hi, lo]).astype(BF16)

    def k_aug(hi, lo):
        width = DIFF_K_WIDTH - 2 * DIFF_QK
        c = lax.broadcasted_iota(jnp.int32, (t, width), 1)
        rem = lax.broadcasted_iota(jnp.int32, (t, width), 0).astype(F32)
        return pick(c, [-POS_BLOCK * hi, -POS_BLOCK * lo, -hi, -lo, blk, blk, rem, rem]).astype(BF16)

    def put_max_sq_norm(row, xt):
        sq = jnp.max(jnp.sum(xt * xt, axis=0, keepdims=True), axis=1, keepdims=True)
        norm_ref[0, row:row + 1, :] = jnp.broadcast_to(sq, (1, norm_ref.shape[2]))

    for h in range(DIFF_HEADS):
        k_maps = []
        for j in range(2):
            r = (2 * h + j) * DIFF_QK
            qj = _rms_cols(ut[o_qd + r:o_qd + r + DIFF_QK], gdq_ref[...]) * d_scale
            kj = _rms_cols(ut[o_kd + r:o_kd + r + DIFF_QK], gdk_ref[...])
            put_max_sq_norm(2 * h + j, qj)
            put_max_sq_norm(n_maps + 2 * h + j, kj)
            qd_ref[0, h, j * DIFF_QK:(j + 1) * DIFF_QK, j * t:(j + 1) * t] = qj.astype(BF16)
            qd_ref[0, h, (1 - j) * DIFF_QK:(2 - j) * DIFF_QK, j * t:(j + 1) * t] = zero_q
            k_maps.append(kj)
        qd_ref[0, h, 2 * DIFF_QK:2 * DIFF_QK + AUG_ROWS, :] = q_aug(*slope_parts[h][1:])
        k12 = jnp.concatenate(k_maps, axis=0)
        kd_ref[0, h, :, 0:2 * DIFF_QK] = k12.T.astype(BF16)
        kd_ref[0, h, :, 2 * DIFF_QK:DIFF_K_WIDTH] = k_aug(*slope_parts[h][1:])
        vd_ref[0, h, 0:DIFF_V, :] = ut[o_vd + h * DIFF_V:o_vd + (h + 1) * DIFF_V].astype(BF16)
        vd_ref[0, h, DIFF_V:DIFF_V + ONES_ROWS, :] = ones


def _key_rows(c, size):
    return pl.ds(pl.multiple_of(c * size, size), size)


def _store_scores(s_buf, max_buf, slot, scores):
    sb = scores.astype(BF16)
    s_buf[slot] = sb
    max_buf[slot] = jnp.max(sb, axis=0, keepdims=True).astype(F32)


def _sweep_head(*, n_tiles, width, window, first_chunk, issue_scores, v_chunk, finalize,
                s_buf, max_buf, p_buf, acc_buf):
    def chunk_of(qi, i, k):
        lo, count = window(qi)
        return lo + (first_chunk(qi) - lo + 2 * i + k) % count

    def step(qi, i, parity, carry, first=False, last=False):
        m, alphas = carry
        cur = [(s_buf[parity, k], max_buf[parity, k]) for k in range(2)]

        nxt_qi, nxt_i = (qi + 1, 0) if last else (qi, i + 1)
        nxt_qi = jnp.minimum(nxt_qi, n_tiles - 1)
        for k in range(2):
            issue_scores(nxt_qi, chunk_of(nxt_qi, nxt_i, k), last and k == 0, (1 - parity, k))

        if first:
            prv_qi = jnp.maximum(qi - 1, 0)
            prv_i = window(prv_qi)[1] // 2 - 1
        else:
            prv_qi, prv_i = qi, i - 1
        for k in range(2):
            acc_buf[...] = (acc_buf[...] * alphas[k]
                            + _dot(v_chunk(chunk_of(prv_qi, prv_i, k)), p_buf[1 - parity, k]))

        if first:
            m = jnp.full((1, width), NEG_BIG, F32)
        alphas = []
        for k, (sb, chunk_max) in enumerate(cur):
            m_new = jnp.maximum(m, chunk_max)
            p_buf[parity, k] = jnp.exp2(sb - m_new.astype(BF16))
            alphas.append(jnp.exp2(m - m_new))
            m = m_new
        return m, tuple(alphas)

    def tile(qi, carry):
        def two_steps(j, cr):
            cr = step(qi, 2 * j + 1, 1, cr)
            return step(qi, 2 * j + 2, 0, cr)
        n_pairs = window(qi)[1] // 2
        carry = lax.fori_loop(0, n_pairs // 2 - 1, two_steps, carry)
        carry = step(qi, n_pairs - 1, 1, carry, last=True)
        carry = step(qi + 1, 0, 0, carry, first=True)
        finalize(qi)
        return carry

    p_buf[...] = jnp.zeros(p_buf.shape, p_buf.dtype)
    acc_buf[...] = jnp.zeros(acc_buf.shape, acc_buf.dtype)
    for k in range(2):
        issue_scores(0, chunk_of(0, 0, k), k == 0, (0, k))
    zero_row = jnp.zeros((1, width), F32)
    carry = (zero_row, (zero_row, zero_row))
    carry = step(0, 0, 0, carry, first=True)
    lax.fori_loop(0, n_tiles, tile, carry)


def _mla_attn_kernel(q_ref, k_ref, v_ref, o_ref, s_buf, max_buf, p_buf, acc_buf):
    chunk, tq = s_buf.shape[2:]
    seq = k_ref.shape[2]

    def tile_cols(qi):
        return pl.ds(pl.multiple_of(qi * tq, tq), tq)

    def issue_scores(qi, c, is_first_chunk, slot):
        _store_scores(s_buf, max_buf, slot,
                      _dot(k_ref[0, 0, _key_rows(c, chunk), :], q_ref[0, 0, :, tile_cols(qi)]))

    def finalize(qi):
        o_ref[0, 0, :, tile_cols(qi)] = (
            acc_buf[0:MLA_V, :] / acc_buf[MLA_V:MLA_V + 1, :]).astype(BF16)

    _sweep_head(n_tiles=seq // tq, width=tq, window=lambda qi: (0, seq // chunk),
                first_chunk=lambda qi: 0, issue_scores=issue_scores,
                v_chunk=lambda c: v_ref[0, 0, :, _key_rows(c, chunk)], finalize=finalize,
                s_buf=s_buf, max_buf=max_buf, p_buf=p_buf, acc_buf=acc_buf)


def _diff_attn_kernel(reach_ref, q_ref, k_ref, v_ref, lq1_ref, lk1_ref, lq2_ref, lk2_ref, gsub_ref,
                      o_ref, s_buf, max_buf, p_buf, acc_buf, *, lambda_init):
    chunk, tq = s_buf.shape[2], s_buf.shape[3] // 2
    seq = k_ref.shape[2]
    n_chunks = seq // chunk
    assert chunk % tq == 0 and n_chunks % 4 == 0
    first_chunk = lambda qi: (qi * tq) // chunk
    reach = reach_ref[pl.program_id(0) * pl.num_programs(1) + pl.program_id(1)]

    def window(qi):
        lo = jnp.maximum(qi * tq - reach, 0) // chunk
        hi = jnp.minimum(qi * tq + tq - 1 + reach, seq - 1) // chunk
        count = jnp.minimum((hi - lo + 4) // 4 * 4, n_chunks)
        return jnp.minimum(lo, n_chunks - count), count

    lam = (jnp.exp(jnp.sum(lq1_ref[...] * lk1_ref[...], axis=-1, keepdims=True))
           - jnp.exp(jnp.sum(lq2_ref[...] * lk2_ref[...], axis=-1, keepdims=True)) + lambda_init)
    zero_rows = jnp.zeros((DIFF_K_WIDTH - 2 * DIFF_QK - AUG_ROWS, 2 * tq), BF16)

    def issue_scores(qi, c, is_first_chunk, slot):
        qa = q_ref[0, 0, :, pl.ds(pl.multiple_of(qi * 2 * tq, 2 * tq), 2 * tq)]
        kc = k_ref[0, 0, _key_rows(c, chunk), :]

        def scores(sign):
            aug = (qa[2 * DIFF_QK:].astype(F32) * sign).astype(BF16)
            return _dot(kc, jnp.concatenate([qa[:2 * DIFF_QK], aug, zero_rows], axis=0))

        if is_first_chunk:
            st = jnp.minimum(scores(1.0), scores(-1.0))
        else:
            st = scores(jnp.where(c < first_chunk(qi), 1.0, -1.0).astype(F32))
        _store_scores(s_buf, max_buf, slot, st)

    def finalize(qi):
        o_all = acc_buf[0:DIFF_V, :] / acc_buf[DIFF_V:DIFF_V + 1, :]
        sub = POS_BLOCK
        o = jnp.concatenate(
            [o_all[:, 2 * j * sub:(2 * j + 1) * sub] - lam * o_all[:, (2 * j + 1) * sub:(2 * j + 2) * sub]
             for j in range(tq // sub)], axis=1)
        o_ref[0, 0, :, pl.ds(pl.multiple_of(qi * tq, tq), tq)] = (
            _rms_cols(o, gsub_ref[...]) * (1.0 - lambda_init)).astype(BF16)

    _sweep_head(n_tiles=seq // tq, width=2 * tq, window=window,
                first_chunk=first_chunk, issue_scores=issue_scores,
                v_chunk=lambda c: v_ref[0, 0, :, _key_rows(c, chunk)], finalize=finalize,
                s_buf=s_buf, max_buf=max_buf, p_buf=p_buf, acc_buf=acc_buf)


def _post_kernel(h1_ref, om_ref, od_ref, p_ref, wo_hbm, g2_ref, wg_hbm, wu_hbm, wd_hbm,
                 gpi_ref, wpg_hbm, bpg_ref, wpp_hbm, gpo_ref, out_ref,
                 wg_ref, wu_ref, wd_ref, stage_in, stage_out, sem, wo_ref, wpg_ref, wpp_ref, stage_sq):
    _load_ffn_weights(wg_hbm, wu_hbm, wd_hbm, wg_ref, wu_ref, wd_ref, stage_in, stage_out, sem)

    @pl.when(pl.program_id(0) == 0)
    def _():
        for src, dst in ((wo_hbm, wo_ref), (wpg_hbm, wpg_ref), (wpp_hbm, wpp_ref)):
            _load_as_bf16(src, dst, stage_sq, sem)

    mla_w = om_ref.shape[1]
    h = h1_ref[...] + _dot_tn(om_ref[0], wo_ref[0:mla_w, :]) + _dot_tn(od_ref[0], wo_ref[mla_w:, :])
    h = h + 0.5 * _swiglu(_rms_rows(h, g2_ref[...]).astype(BF16), wg_ref, wu_ref, wd_ref)
    z = _dot(_rms_rows(h, gpi_ref[...]).astype(BF16), wpg_ref[...]) + bpg_ref[...]
    gate = 1.0 / (1.0 + jnp.exp(-z))
    emb = _rms_rows(_dot(p_ref[...].astype(BF16), wpp_ref[...]), gpo_ref[...])
    out_ref[...] = h + gate * emb


def _resident(shape):
    nd = len(shape)
    return pl.BlockSpec(shape, lambda *_: (0,) * nd, pipeline_mode=pl.Buffered(1))


def _params(n_axes):
    return pltpu.CompilerParams(dimension_semantics=("arbitrary",) * n_axes,
                                vmem_limit_bytes=VMEM_LIMIT_BYTES)


def _ffn_weight_scratch(d_model, d_ff):
    return [pltpu.VMEM((d_model, d_ff), BF16), pltpu.VMEM((d_model, d_ff), BF16),
            pltpu.VMEM((d_ff, d_model), BF16),
            pltpu.VMEM((2, d_model // FFN_STAGE_BLOCKS, d_ff), F32),
            pltpu.VMEM((2, d_ff // FFN_STAGE_BLOCKS, d_model), F32),
            pltpu.SemaphoreType.DMA((2,))]


def _attn_scratch(chunk, width, rows):
    return [pltpu.VMEM((2, 2, chunk, width), BF16), pltpu.VMEM((2, 2, 1, width), F32),
            pltpu.VMEM((2, 2, chunk, width), BF16), pltpu.VMEM((rows, width), F32)]


def _rope_tables_t(seq):
    pos = jnp.arange(seq, dtype=F32)
    inv = ROPE_THETA ** (-jnp.arange(0, MLA_ROPE, 2, dtype=F32) / MLA_ROPE)
    ang = pos[:, None] * inv[None, :]
    ang = jnp.concatenate([ang, ang], axis=-1)
    return jnp.cos(ang).T, jnp.sin(ang).T


def _alibi_reach(sq_norms, slopes2, batch, seq):
    n_maps = 2 * DIFF_HEADS
    top = jnp.max(sq_norms[:, :, 0].reshape(batch, -1, 2 * n_maps), axis=1)
    bound = jnp.sqrt(jnp.max((top[:, :n_maps] * top[:, n_maps:]).reshape(batch, DIFF_HEADS, 2), axis=-1))
    bound = bound * 1.02
    reach = jnp.ceil((2.0 * bound + WEIGHT_CUTOFF_BITS) / slopes2[None, :])
    return jnp.minimum(reach, float(seq)).astype(jnp.int32).reshape(-1)


def _alibi_slope_parts():
    parts = []
    for h in range(DIFF_HEADS):
        s2 = np.float32(LOG2E * 2.0 ** (-8.0 * (h + 1) / DIFF_HEADS))
        bits = s2.view(np.uint32)
        hi = np.uint32((bits + np.uint32(0x7FFF) + ((bits >> np.uint32(16)) & np.uint32(1)))
                       & np.uint32(0xFFFF0000)).view(np.float32)
        parts.append((float(s2), float(hi), float(s2 - hi)))
    return tuple(parts)


def kernel(x, p, g_ffn1, w_ffn1_gate, w_ffn1_up, w_ffn1_down, g_mix, w_in, g_q_lat, w_q_up, g_kv_lat, w_kv_up, g_mla_q, g_mla_k, g_diff_q, g_diff_k, lambda_q1, lambda_k1, lambda_q2, lambda_k2, g_diff_sub, w_out, g_ffn2, w_ffn2_gate, w_ffn2_up, w_ffn2_down, g_ple_in, w_ple_gate, b_ple_gate, w_ple_proj, g_ple_out):
    batch, seq, d_model = x.shape
    depth = p.shape[0]
    tokens = batch * seq
    tm = TOKEN_TILE
    tiles_per_seq = seq // tm
    assert seq % tm == 0 and seq % POST_TOKEN_TILE == 0 and seq % MLA_Q_TILE == 0
    assert seq % (4 * MLA_KV_CHUNK) == 0 and seq % (4 * DIFF_KV_CHUNK) == 0
    assert tm == POS_BLOCK and DIFF_Q_TILE % POS_BLOCK == 0 and seq % DIFF_Q_TILE == 0

    cos_t, sin_t = _rope_tables_t(seq)
    slope_parts = _alibi_slope_parts()
    slopes2 = jnp.asarray([s2 for s2, _, _ in slope_parts], F32)
    row = lambda v: v.reshape(1, -1)
    col = lambda v: v.reshape(-1, 1)
    bf = lambda w: w.astype(BF16)

    h = x.reshape(tokens, d_model)
    for i in range(depth):
        lambda_init = 0.8 - 0.6 * math.exp(-0.3 * i)
        ple_dim = p.shape[-1]

        in_hbm = pl.BlockSpec(memory_space=pl.ANY)
        d_ff = w_ffn1_gate.shape[-1]
        tok_spec = pl.BlockSpec((tm, d_model), lambda t: (t, 0))
        head_t = lambda heads, rows, width: pl.BlockSpec(
            (1, heads, rows, width), lambda t: (t // tiles_per_seq, 0, 0, t % tiles_per_seq))
        head_r = lambda heads, width: pl.BlockSpec(
            (1, heads, tm, width), lambda t: (t // tiles_per_seq, 0, t % tiles_per_seq, 0))
        rope_spec = pl.BlockSpec((MLA_ROPE, tm), lambda t: (0, t % tiles_per_seq))
        pre_inputs = [
            (h, tok_spec),
            (row(g_ffn1[i]), None), (w_ffn1_gate[i], in_hbm), (w_ffn1_up[i], in_hbm),
            (w_ffn1_down[i], in_hbm), (row(g_mix[i]), None), (bf(w_in[i].T), None),
            (col(g_q_lat[i]), None), (bf(w_q_up[i].T), None),
            (col(g_kv_lat[i]), None), (bf(w_kv_up[i].T), None),
            (col(g_mla_q[i, :MLA_NOPE]), None), (col(g_mla_q[i, MLA_NOPE:]), None),
            (col(g_mla_k[i, :MLA_NOPE]), None), (col(g_mla_k[i, MLA_NOPE:]), None),
            (col(g_diff_q[i]), None), (col(g_diff_k[i]), None),
            (cos_t, rope_spec), (sin_t, rope_spec),
        ]
        dq_rows = 2 * DIFF_QK + AUG_ROWS
        h1, qm, km, vm, qd, kd, vd, sq_norms = pl.pallas_call(
            functools.partial(_pre_kernel, slope_parts=slope_parts, tiles_per_seq=tiles_per_seq),
            grid=(tokens // tm,),
            in_specs=[s if s is not None else _resident(a.shape) for a, s in pre_inputs],
            out_specs=[tok_spec, head_t(MLA_HEADS, QK_PAD, tm), head_r(MLA_HEADS, QK_PAD),
                       head_t(MLA_HEADS, MLA_V + ONES_ROWS, tm), head_t(DIFF_HEADS, dq_rows, 2 * tm),
                       head_r(DIFF_HEADS, DIFF_K_WIDTH), head_t(DIFF_HEADS, DIFF_V + ONES_ROWS, tm),
                       pl.BlockSpec((1, 4 * DIFF_HEADS, LANES), lambda t: (t, 0, 0))],
            out_shape=[
                jax.ShapeDtypeStruct((tokens, d_model), F32),
                jax.ShapeDtypeStruct((batch, MLA_HEADS, QK_PAD, seq), BF16),
                jax.ShapeDtypeStruct((batch, MLA_HEADS, seq, QK_PAD), BF16),
                jax.ShapeDtypeStruct((batch, MLA_HEADS, MLA_V + ONES_ROWS, seq), BF16),
                jax.ShapeDtypeStruct((batch, DIFF_HEADS, dq_rows, 2 * seq), BF16),
                jax.ShapeDtypeStruct((batch, DIFF_HEADS, seq, DIFF_K_WIDTH), BF16),
                jax.ShapeDtypeStruct((batch, DIFF_HEADS, DIFF_V + ONES_ROWS, seq), BF16),
                jax.ShapeDtypeStruct((tokens // tm, 4 * DIFF_HEADS, LANES), F32),
            ],
            scratch_shapes=_ffn_weight_scratch(d_model, d_ff),
            compiler_params=_params(1),
            name="pre",
        )(*[a for a, _ in pre_inputs])

        per_head = lambda *tail: pl.BlockSpec((1, 1) + tail, lambda b, hh: (b, hh, 0, 0))
        om = pl.pallas_call(
            _mla_attn_kernel,
            grid=(batch, MLA_HEADS),
            in_specs=[per_head(QK_PAD, seq), per_head(seq, QK_PAD), per_head(MLA_V + ONES_ROWS, seq)],
            out_specs=per_head(MLA_V, seq),
            out_shape=jax.ShapeDtypeStruct((batch, MLA_HEADS, MLA_V, seq), BF16),
            scratch_shapes=_attn_scratch(MLA_KV_CHUNK, MLA_Q_TILE, MLA_V + ONES_ROWS),
            compiler_params=_params(2),
            name="mla_attn",
        )(qm, km, vm)

        small = lambda shape: pl.BlockSpec(shape, lambda b, hh: (0,) * len(shape))
        od = pl.pallas_call(
            functools.partial(_diff_attn_kernel, lambda_init=lambda_init),
            grid=(batch, DIFF_HEADS),
            in_specs=[pl.BlockSpec(memory_space=pltpu.SMEM),
                      per_head(dq_rows, 2 * seq), per_head(seq, DIFF_K_WIDTH),
                      per_head(DIFF_V + ONES_ROWS, seq)]
                     + [small((1, DIFF_QK))] * 4 + [small((DIFF_V, 1))],
            out_specs=per_head(DIFF_V, seq),
            out_shape=jax.ShapeDtypeStruct((batch, DIFF_HEADS, DIFF_V, seq), BF16),
            scratch_shapes=_attn_scratch(DIFF_KV_CHUNK, 2 * DIFF_Q_TILE, DIFF_V + ONES_ROWS),
            compiler_params=_params(2),
            name="diff_attn",
        )(_alibi_reach(sq_norms, slopes2, batch, seq), qd, kd, vd, row(lambda_q1[i]), row(lambda_k1[i]), row(lambda_q2[i]),
          row(lambda_k2[i]), col(g_diff_sub[i]))

        mla_w = MLA_HEADS * MLA_V
        diff_w = DIFF_HEADS * DIFF_V
        tp = POST_TOKEN_TILE
        post_tiles_per_seq = seq // tp
        post_tok_spec = pl.BlockSpec((tp, d_model), lambda t: (t, 0))
        feat_t = lambda width: pl.BlockSpec(
            (1, width, tp), lambda t: (t // post_tiles_per_seq, 0, t % post_tiles_per_seq))
        post_inputs = [
            (h1, post_tok_spec),
            (om.reshape(batch, mla_w, seq), feat_t(mla_w)),
            (od.reshape(batch, diff_w, seq), feat_t(diff_w)),
            (p[i].reshape(tokens, ple_dim), pl.BlockSpec((tp, ple_dim), lambda t: (t, 0))),
            (w_out[i], in_hbm),
            (row(g_ffn2[i]), None), (w_ffn2_gate[i], in_hbm), (w_ffn2_up[i], in_hbm),
            (w_ffn2_down[i], in_hbm),
            (row(g_ple_in[i]), None), (w_ple_gate[i], in_hbm), (row(b_ple_gate[i]), None),
            (w_ple_proj[i], in_hbm), (row(g_ple_out[i]), None),
        ]
        post_scratch = _ffn_weight_scratch(d_model, d_ff) + [
            pltpu.VMEM(w_out.shape[1:], BF16), pltpu.VMEM(w_ple_gate.shape[1:], BF16),
            pltpu.VMEM(w_ple_proj.shape[1:], BF16), pltpu.VMEM((2, 128, d_model), F32)]
        h = pl.pallas_call(
            _post_kernel,
            grid=(tokens // tp,),
            in_specs=[s if s is not None else _resident(a.shape) for a, s in post_inputs],
            out_specs=post_tok_spec,
            out_shape=jax.ShapeDtypeStruct((tokens, d_model), F32),
            scratch_shapes=post_scratch,
            compiler_params=_params(1),
            name="post",
        )(*[a for a, _ in post_inputs])

    return h.reshape(batch, seq, d_model)
```
